```python
import jax, jax.numpy as jnp
from jax import lax
import numpy as np

D_MODEL = 4096
BATCH = 4
SEQ = 2048
DEPTH = 2
DEC_BATCH = 128
DEC_SEQ = 1
PAST_LEN = 16384
PAGE_SIZE = 128

N_EVEN = (DEPTH + 1) // 2
N_ODD = DEPTH // 2
POOL_WIDTH = D_MODEL // 2
POOL_WINDOWS = (2, 4, 8, 16)
POOL_GROUPS = len(POOL_WINDOWS)
POOL_GW = POOL_WIDTH // POOL_GROUPS
POOL_PREFIX = max(POOL_WINDOWS) - 1
SCONV_WIDTH = D_MODEL // 2
SCONV_K = 3
CONF_WIDTH = D_MODEL // 2
CONF_K = 31
MLA_HEADS = 32
Q_LORA = 1024
KV_LORA = 512
QK_NOPE = 128
QK_ROPE = 64
V_HEAD = 128
ROPE_THETA = 10000.0
Q_BLOCK = 128
SM_SCALE = (QK_NOPE + QK_ROPE) ** -0.5
NEG_INF = -1e30
D_FF = 11008
N_EXPERTS = 8
TOP_K = 2
D_EXPERT = 14336
EPS = 1e-6
IN0 = POOL_WIDTH + 3 * SCONV_WIDTH
MIX0 = POOL_WIDTH + SCONV_WIDTH
IN1 = 2 * CONF_WIDTH + Q_LORA + KV_LORA + QK_ROPE
MIX1 = CONF_WIDTH + MLA_HEADS * V_HEAD

kernel_name = 'hybrid_pool_sconv_conformer_mla_moe_step'


def rms_norm(x, g):
    x32 = x.astype(jnp.float32)
    y = x32 * lax.rsqrt(jnp.mean(x32 * x32, axis=-1, keepdims=True) + EPS)
    return (y * g.astype(jnp.float32)).astype(x.dtype)


def layer_norm(x, g, b):
    x32 = x.astype(jnp.float32)
    mu = jnp.mean(x32, axis=-1, keepdims=True)
    var = jnp.mean(jnp.square(x32 - mu), axis=-1, keepdims=True)
    y = (x32 - mu) * lax.rsqrt(var + EPS)
    return (y * g.astype(jnp.float32) + b.astype(jnp.float32)).astype(x.dtype)


def rope_angles(pos):
    inv = ROPE_THETA ** (-jnp.arange(QK_ROPE // 2, dtype=jnp.float32) * (2.0 / QK_ROPE))
    ang = pos.astype(jnp.float32)[:, None] * inv[None, :]
    return jnp.cos(ang), jnp.sin(ang)


def apply_rope(x, cos, sin):
    x32 = x.astype(jnp.float32)
    x1, x2 = x32[..., :QK_ROPE // 2], x32[..., QK_ROPE // 2:]
    return jnp.concatenate([x1 * cos - x2 * sin, x1 * sin + x2 * cos], axis=-1).astype(x.dtype)


def causal_dwconv(v_ext, w):
    return lax.conv_general_dilated(v_ext, w[:, None, :].astype(v_ext.dtype), window_strides=(1,), padding='VALID', dimension_numbers=('NWC', 'WIO', 'NWC'), feature_group_count=w.shape[1])


def pool_mix(u_ext, pos, w_pool, pool_scale):
    n, l, _ = u_ext.shape
    t = l - POOL_PREFIX
    cs = jnp.cumsum(u_ext.astype(jnp.float32), axis=1)
    cs = jnp.pad(cs, ((0, 0), (1, 0), (0, 0)))
    end = cs[:, POOL_PREFIX + 1:]
    u_new = u_ext[:, POOL_PREFIX:].astype(jnp.float32)
    diffs = []
    for g, w in enumerate(POOL_WINDOWS):
        ch = slice(g * POOL_GW, (g + 1) * POOL_GW)
        start = cs[:, POOL_PREFIX + 1 - w:POOL_PREFIX + 1 - w + t, ch]
        count = jnp.minimum(pos + 1, w).astype(jnp.float32)[None, :, None]
        diffs.append((end[..., ch] - start) / count - u_new[..., ch])
    d = jnp.stack(diffs, axis=2).astype(u_ext.dtype)
    out = jnp.einsum('ntgc,gcd->ntgd', d, w_pool).reshape(n, t, POOL_WIDTH)
    return out * pool_scale


def even_mixer(h, pool_prev, sconv_prev, pos, w_in, w_pool, pool_scale, sconv_w, w_out):
    proj = h @ w_in
    a, b, c, v = jnp.split(proj, [POOL_WIDTH, POOL_WIDTH + SCONV_WIDTH, POOL_WIDTH + 2 * SCONV_WIDTH], axis=-1)
    u_ext = jnp.concatenate([pool_prev.astype(a.dtype), a], axis=1)
    z_ext = jnp.concatenate([sconv_prev.astype(c.dtype), c * v], axis=1)
    y_a = pool_mix(u_ext, pos, w_pool, pool_scale)
    y_b = b * causal_dwconv(z_ext, sconv_w)
    y = jnp.concatenate([y_a.astype(h.dtype), y_b], axis=-1) @ w_out
    return y, u_ext[:, -POOL_PREFIX:], z_ext[:, -(SCONV_K - 1):]


def odd_project(h, conf_prev, cos, sin, w_in, conf_w, conf_b, ln_g, ln_b, q_norm_g, w_uq, kv_norm_g):
    proj = h @ w_in
    o1 = CONF_WIDTH
    o2 = 2 * CONF_WIDTH
    o3 = o2 + Q_LORA
    o4 = o3 + KV_LORA
    ga, gg, qa, kva, kpe = jnp.split(proj, [o1, o2, o3, o4], axis=-1)
    glu = ga * jax.nn.sigmoid(gg)
    g_ext = jnp.concatenate([conf_prev.astype(glu.dtype), glu], axis=1)
    cv = causal_dwconv(g_ext, conf_w) + conf_b
    c_out = jax.nn.silu(layer_norm(cv, ln_g, ln_b))
    q = jnp.einsum('ntq,qhd->nthd', rms_norm(qa, q_norm_g), w_uq)
    q_nope = q[..., :QK_NOPE]
    q_pe = apply_rope(q[..., QK_NOPE:], cos[None, :, None, :], sin[None, :, None, :])
    ckv = rms_norm(kva, kv_norm_g)
    kpe = apply_rope(kpe, cos[None], sin[None])
    return c_out, q_nope, q_pe, ckv, kpe, g_ext[:, -(CONF_K - 1):]


def latent_attend(q_nope, q_pe, keys_c, keys_pe, mask, w_uk):
    q_lat = jnp.einsum('...qhd,chd->...qhc', q_nope, w_uk)
    s = jnp.einsum('...qhc,...kc->...hqk', q_lat, keys_c, preferred_element_type=jnp.float32)
    s = s + jnp.einsum('...qhr,...kr->...hqk', q_pe, keys_pe, preferred_element_type=jnp.float32)
    s = jnp.where(mask, s * SM_SCALE, NEG_INF)
    p = jax.nn.softmax(s, axis=-1)
    return jnp.einsum('...hqk,...kc->...qhc', p.astype(keys_c.dtype), keys_c)


def prompt_attention(q_nope, q_pe, ckv, kpe, w_uk):
    n, s = q_nope.shape[:2]
    nb = s // Q_BLOCK
    qn = q_nope.reshape(n, nb, Q_BLOCK, MLA_HEADS, QK_NOPE).swapaxes(0, 1)
    qp = q_pe.reshape(n, nb, Q_BLOCK, MLA_HEADS, QK_ROPE).swapaxes(0, 1)
    kpos = jnp.arange(s)

    def one_block(args):
        bi, qn_b, qp_b = args
        qpos = bi * Q_BLOCK + jnp.arange(Q_BLOCK)
        mask = kpos[None, :] <= qpos[:, None]
        return latent_attend(qn_b, qp_b, ckv, kpe, mask, w_uk)

    o = lax.map(one_block, (jnp.arange(nb), qn, qp))
    return o.swapaxes(0, 1).reshape(n, s, MLA_HEADS, KV_LORA)


def sample_attention(q_nope, q_pe, ckv_new, kpe_new, cache_ckv, cache_kpe, layer, page_table, w_uk):
    t = q_nope.shape[1]
    past = page_table.shape[1] * PAGE_SIZE
    kpos = jnp.arange(past + t)
    qpos = past + jnp.arange(t)
    mask = kpos[None, :] <= qpos[:, None]

    def one_sequence(args):
        pt, qn, qp, cn, pn = args
        kc = jnp.concatenate([cache_ckv[layer, pt].reshape(past, KV_LORA).astype(cn.dtype), cn], axis=0)
        kp = jnp.concatenate([cache_kpe[layer, pt].reshape(past, QK_ROPE).astype(pn.dtype), pn], axis=0)
        return latent_attend(qn, qp, kc, kp, mask, w_uk)

    return lax.map(one_sequence, (page_table, q_nope, q_pe, ckv_new, kpe_new))


def odd_output(c_out, o_lat, w_uv, w_out):
    n, t = c_out.shape[:2]
    v = jnp.einsum('nthc,chd->nthd', o_lat, w_uv).reshape(n, t, MLA_HEADS * V_HEAD)
    return jnp.concatenate([c_out, v.astype(c_out.dtype)], axis=-1) @ w_out


def swiglu(x, w1, w3, w2):
    return (jax.nn.silu(x @ w1) * (x @ w3)) @ w2


def moe_ffn(x, w_router, b_router, w1, w3, w2, layer):
    logits = jnp.einsum('ntd,de->nte', x, w_router, preferred_element_type=jnp.float32) + b_router.astype(jnp.float32)
    top_v, top_i = lax.top_k(logits, TOP_K)
    gates = jax.nn.softmax(top_v, axis=-1)
    g = jnp.sum(jax.nn.one_hot(top_i, N_EXPERTS, dtype=jnp.float32) * gates[..., None], axis=-2)
    y = jnp.zeros_like(x)
    for e in range(N_EXPERTS):
        y = y + g[..., e:e + 1].astype(x.dtype) * swiglu(x, w1[layer, e], w3[layer, e], w2[layer, e])
    return y


def setup_inputs(seed: int = 0) -> dict:
    key = jax.random.key(seed)
    ks = iter(jax.random.split(key, 48))
    f32 = jnp.float32

    def nrm(shape, scale=1.0):
        return jax.random.normal(next(ks), shape, f32) * scale

    def gain(shape):
        return 1.0 + 0.02 * jax.random.normal(next(ks), shape, f32)

    n_pages = PAST_LEN // PAGE_SIZE
    n_used = DEC_BATCH * n_pages
    n_pool = n_used + n_used // 4
    d = D_MODEL
    return {
        'x_prompt': nrm((BATCH, SEQ, d)),
        'x_sample': nrm((DEC_BATCH, DEC_SEQ, d)),
        'state_pool': nrm((N_EVEN, DEC_BATCH, POOL_PREFIX, POOL_WIDTH)),
        'state_sconv': nrm((N_EVEN, DEC_BATCH, SCONV_K - 1, SCONV_WIDTH)),
        'state_conf': nrm((N_ODD, DEC_BATCH, CONF_K - 1, CONF_WIDTH), 0.5),
        'cache_ckv': nrm((N_ODD, n_pool, PAGE_SIZE, KV_LORA)),
        'cache_kpe': nrm((N_ODD, n_pool, PAGE_SIZE, QK_ROPE)),
        'page_table': jax.random.permutation(next(ks), n_pool)[:n_used].reshape(DEC_BATCH, n_pages).astype(jnp.int32),
        'norm_mix_pre': gain((DEPTH, d)),
        'norm_mix_post': gain((DEPTH, d)),
        'norm_ffn_pre': gain((DEPTH, d)),
        'norm_ffn_post': gain((DEPTH, d)),
        'w_in0': nrm((N_EVEN, d, IN0), d ** -0.5),
        'w_pool': nrm((N_EVEN, POOL_GROUPS, POOL_GW, POOL_GW), POOL_GW ** -0.5),
        'pool_scale': gain((N_EVEN, POOL_WIDTH)),
        'sconv_w': nrm((N_EVEN, SCONV_K, SCONV_WIDTH), SCONV_K ** -0.5),
        'w_out0': nrm((N_EVEN, MIX0, d), MIX0 ** -0.5),
        'w_in1': nrm((N_ODD, d, IN1), d ** -0.5),
        'conf_w': nrm((N_ODD, CONF_K, CONF_WIDTH), CONF_K ** -0.5),
        'conf_b': nrm((N_ODD, CONF_WIDTH), 0.02),
        'conf_ln_g': gain((N_ODD, CONF_WIDTH)),
        'conf_ln_b': nrm((N_ODD, CONF_WIDTH), 0.02),
        'q_norm_g': gain((N_ODD, Q_LORA)),
        'w_uq': nrm((N_ODD, Q_LORA, MLA_HEADS, QK_NOPE + QK_ROPE), Q_LORA ** -0.5),
        'kv_norm_g': gain((N_ODD, KV_LORA)),
        'w_uk': nrm((N_ODD, KV_LORA, MLA_HEADS, QK_NOPE), KV_LORA ** -0.5),
        'w_uv': nrm((N_ODD, KV_LORA, MLA_HEADS, V_HEAD), KV_LORA ** -0.5),
        'w_out1': nrm((N_ODD, MIX1, d), MIX1 ** -0.5),
        'ffn_w1': nrm((N_EVEN, d, D_FF), d ** -0.5),
        'ffn_w3': nrm((N_EVEN, d, D_FF), d ** -0.5),
        'ffn_w2': nrm((N_EVEN, D_FF, d), D_FF ** -0.5),
        'router_w': nrm((N_ODD, d, N_EXPERTS), d ** -0.5),
        'router_b': nrm((N_ODD, N_EXPERTS), 0.01),
        'moe_w1': nrm((N_ODD, N_EXPERTS, d, D_EXPERT), d ** -0.5),
        'moe_w3': nrm((N_ODD, N_EXPERTS, d, D_EXPERT), d ** -0.5),
        'moe_w2': nrm((N_ODD, N_EXPERTS, D_EXPERT, d), D_EXPERT ** -0.5),
    }


def reference(x_prompt, x_sample, state_pool, state_sconv, state_conf, cache_ckv, cache_kpe, page_table,
              norm_mix_pre, norm_mix_post, norm_ffn_pre, norm_ffn_post,
              w_in0, w_pool, pool_scale, sconv_w, w_out0,
              w_in1, conf_w, conf_b, conf_ln_g, conf_ln_b, q_norm_g, w_uq, kv_norm_g, w_uk, w_uv, w_out1,
              ffn_w1, ffn_w3, ffn_w2,
              router_w, router_b, moe_w1, moe_w3, moe_w2):
    n_p, s_p = x_prompt.shape[:2]
    t_s = x_sample.shape[1]
    pos_p = jnp.arange(s_p)
    pos_s = PAST_LEN + jnp.arange(t_s)
    cos_p, sin_p = rope_angles(pos_p)
    cos_s, sin_s = rope_angles(pos_s)
    xp, xs = x_prompt, x_sample
    pool_p, pool_s, sconv_p, sconv_s = [], [], [], []
    conf_p, conf_s, ckv_p, ckv_s, kpe_p, kpe_s = [], [], [], [], [], []
    for l in range(DEPTH):
        i = l // 2
        hp = rms_norm(xp, norm_mix_pre[l])
        hs = rms_norm(xs, norm_mix_pre[l])
        if l % 2 == 0:
            ew = (w_in0[i], w_pool[i], pool_scale[i], sconv_w[i], w_out0[i])
            yp, st_a, st_b = even_mixer(hp, jnp.zeros((n_p, POOL_PREFIX, POOL_WIDTH), hp.dtype), jnp.zeros((n_p, SCONV_K - 1, SCONV_WIDTH), hp.dtype), pos_p, *ew)
            pool_p.append(st_a)
            sconv_p.append(st_b)
            ys, st_a, st_b = even_mixer(hs, state_pool[i], state_sconv[i], pos_s, *ew)
            pool_s.append(st_a)
            sconv_s.append(st_b)
        else:
            ow = (w_in1[i], conf_w[i], conf_b[i], conf_ln_g[i], conf_ln_b[i], q_norm_g[i], w_uq[i], kv_norm_g[i])
            c_o, qn, qr, kv, kr, st_c = odd_project(hp, jnp.zeros((n_p, CONF_K - 1, CONF_WIDTH), hp.dtype), cos_p, sin_p, *ow)
            o_lat = prompt_attention(qn, qr, kv, kr, w_uk[i])
            yp = odd_output(c_o, o_lat, w_uv[i], w_out1[i])
            conf_p.append(st_c)
            ckv_p.append(kv)
            kpe_p.append(kr)
            c_o, qn, qr, kv, kr, st_c = odd_project(hs, state_conf[i], cos_s, sin_s, *ow)
            o_lat = sample_attention(qn, qr, kv, kr, cache_ckv, cache_kpe, i, page_table, w_uk[i])
            ys = odd_output(c_o, o_lat, w_uv[i], w_out1[i])
            conf_s.append(st_c)
            ckv_s.append(kv)
            kpe_s.append(kr)
        xp = xp + rms_norm(yp, norm_mix_post[l])
        xs = xs + rms_norm(ys, norm_mix_post[l])
        hp = rms_norm(xp, norm_ffn_pre[l])
        hs = rms_norm(xs, norm_ffn_pre[l])
        if l % 2 == 0:
            fp = swiglu(hp, ffn_w1[i], ffn_w3[i], ffn_w2[i])
            fs = swiglu(hs, ffn_w1[i], ffn_w3[i], ffn_w2[i])
        else:
            fp = moe_ffn(hp, router_w[i], router_b[i], moe_w1, moe_w3, moe_w2, i)
            fs = moe_ffn(hs, router_w[i], router_b[i], moe_w1, moe_w3, moe_w2, i)
        xp = xp + rms_norm(fp, norm_ffn_post[l])
        xs = xs + rms_norm(fs, norm_ffn_post[l])
    return (xp, xs,
            jnp.stack(pool_p), jnp.stack(pool_s),
            jnp.stack(sconv_p), jnp.stack(sconv_s),
            jnp.stack(conf_p), jnp.stack(conf_s),
            jnp.stack(ckv_p), jnp.stack(ckv_s),
            jnp.stack(kpe_p), jnp.stack(kpe_s))
```

```python
import functools
import math

import jax
import jax.numpy as jnp
from jax import lax
from jax.experimental import pallas as pl
from jax.experimental.pallas import tpu as pltpu

F32 = jnp.float32
BF16 = jnp.bfloat16

EPS = 1e-6
ROPE_THETA = 10000.0
NEG_INF = -1e30
POOL_WINDOWS = (2, 4, 8, 16)

V7X_VMEM_BYTES = 64 * 1024 * 1024
LANES = 128
SUBLANES = 8
VMEM_LIMIT = V7X_VMEM_BYTES - 8 * 1024 * 1024

POOL_HALO = 16
SCONV_HALO = 8
CONF_HALO = 32
MOE_TILE_ROWS = 1088


def _params(*sem):
    return pltpu.CompilerParams(dimension_semantics=sem, vmem_limit_bytes=VMEM_LIMIT)


def _div_tile(n, target, align):
    best = None
    for t in range(align, min(n, target) + 1, align):
        if n % t == 0:
            best = t
    return n if best is None else best


def _rms(x, g):
    return x * lax.rsqrt(jnp.mean(x * x, axis=-1, keepdims=True) + EPS) * g


def _rms_cast_kernel(x_ref, g_ref, o_ref):
    o_ref[...] = _rms(x_ref[...], g_ref[...]).astype(o_ref.dtype)


def _rms_cast(x, g, *, width=None, col_block=0, out_dtype=BF16):
    m = x.shape[0]
    width = x.shape[1] if width is None else width
    tr = _div_tile(m, 256, 16)
    return pl.pallas_call(
        _rms_cast_kernel,
        grid=(m // tr,),
        in_specs=[pl.BlockSpec((tr, width), lambda i: (i, col_block)),
                  pl.BlockSpec((1, width), lambda i: (0, 0))],
        out_specs=pl.BlockSpec((tr, width), lambda i: (i, 0)),
        out_shape=jax.ShapeDtypeStruct((m, width), out_dtype),
        compiler_params=_params("parallel"),
        name="rms_cast",
    )(x, g.reshape(1, width))


def _resid_norm_kernel(x_ref, y_ref, gpost_ref, gpre_ref, xo_ref, h_ref):
    xn = x_ref[...] + _rms(y_ref[...], gpost_ref[...])
    xo_ref[...] = xn
    h_ref[...] = _rms(xn, gpre_ref[...]).astype(h_ref.dtype)


def _resid_norm(x, y, g_post, g_pre, h_dtype):
    m, d = x.shape
    tr = _div_tile(m, 256, 16)
    row = pl.BlockSpec((tr, d), lambda i: (i, 0))
    vec = pl.BlockSpec((1, d), lambda i: (0, 0))
    return pl.pallas_call(
        _resid_norm_kernel,
        grid=(m // tr,),
        in_specs=[row, row, vec, vec],
        out_specs=[row, row],
        out_shape=[jax.ShapeDtypeStruct((m, d), F32), jax.ShapeDtypeStruct((m, d), h_dtype)],
        compiler_params=_params("parallel"),
        name="resid_norm",
    )(x, y, g_post.reshape(1, d), g_pre.reshape(1, d))


def _mm_kernel(*refs, n_a):
    a_refs, w_ref, o_ref = refs[:n_a], refs[n_a], refs[n_a + 1]
    acc, off = None, 0
    for a_ref in a_refs:
        k = a_ref.shape[1]
        d = jnp.dot(a_ref[...], w_ref[off:off + k, :].astype(BF16), preferred_element_type=F32)
        acc = d if acc is None else acc + d
        off += k
    o_ref[...] = acc.astype(o_ref.dtype)


def _matmul(a_list, w, *, lead=(), n_cols=None, tn=512, tm_target=1040, out_dtype=F32, name="matmul"):
    m = a_list[0].shape[0]
    k_total = sum(a.shape[1] for a in a_list)
    assert w.shape[len(lead)] == k_total
    n = w.shape[-1] if n_cols is None else n_cols
    tn = min(tn, n)
    assert n % tn == 0
    tm = _div_tile(m, tm_target, 16)
    nl = len(lead)
    in_specs = [pl.BlockSpec((tm, a.shape[1]), lambda i, j: (i, 0)) for a in a_list]
    in_specs.append(pl.BlockSpec((None,) * nl + (k_total, tn), lambda i, j: lead + (0, j)))
    return pl.pallas_call(
        functools.partial(_mm_kernel, n_a=len(a_list)),
        grid=(m // tm, n // tn),
        in_specs=in_specs,
        out_specs=pl.BlockSpec((tm, tn), lambda i, j: (i, j)),
        out_shape=jax.ShapeDtypeStruct((m, n), out_dtype),
        compiler_params=_params("parallel", "arbitrary"),
        name=name,
    )(*a_list, w)


def _silu(x):
    return x * jax.nn.sigmoid(x)


def _gateup_body(a, w1_ref, w3_ref, o_ref, col0, n_valid):
    g = jnp.dot(a, w1_ref[...].astype(BF16), preferred_element_type=F32)
    u = jnp.dot(a, w3_ref[...].astype(BF16), preferred_element_type=F32)
    act = _silu(g) * u
    if n_valid is not None:
        col = col0 + lax.broadcasted_iota(jnp.int32, act.shape, 1)
        act = jnp.where(col < n_valid, act, 0.0)
    o_ref[...] = act.astype(o_ref.dtype)


def _gateup_kernel(a_ref, w1_ref, w3_ref, o_ref, *, tn, n_valid):
    _gateup_body(a_ref[...], w1_ref, w3_ref, o_ref, pl.program_id(1) * tn, n_valid)


def _gateup(a, w1, w3, *, lead, tn, pad_to, tm_target):
    m, k = a.shape
    n = w1.shape[-1]
    assert pad_to % tn == 0
    n_pad = pl.cdiv(n, pad_to) * pad_to
    n_blocks = n_pad // tn
    last_w_block = pl.cdiv(n, tn) - 1
    tm = _div_tile(m, tm_target, 16)
    nl = len(lead)
    wspec = pl.BlockSpec((None,) * nl + (k, tn), lambda i, j: lead + (0, jnp.minimum(j, last_w_block)))
    return pl.pallas_call(
        functools.partial(_gateup_kernel, tn=tn, n_valid=None if n_pad == n else n),
        grid=(m // tm, n_blocks),
        in_specs=[pl.BlockSpec((tm, k), lambda i, j: (i, 0)), wspec, wspec],
        out_specs=pl.BlockSpec((tm, tn), lambda i, j: (i, j)),
        out_shape=jax.ShapeDtypeStruct((m, n_pad), BF16),
        compiler_params=_params("parallel", "arbitrary"),
        name="ffn_gateup",
    )(a, w1, w3)


def _down_body(a_ref, w_ref, o_ref, k_step, tk, k_valid):
    w = w_ref[...]
    if k_valid is not None:
        row = k_step * tk + lax.broadcasted_iota(jnp.int32, w.shape, 0)
        w = jnp.where(row < k_valid, w, 0.0)
    d = jnp.dot(a_ref[...], w.astype(BF16), preferred_element_type=F32)

    @pl.when(k_step == 0)
    def _():
        o_ref[...] = d

    @pl.when(k_step > 0)
    def _():
        o_ref[...] += d


def _down_kernel(a_ref, w_ref, o_ref, *, tk, k_valid):
    _down_body(a_ref, w_ref, o_ref, pl.program_id(2), tk, k_valid)


def _down(a, w, *, lead, tn, tk, tm_target):
    m, k_pad = a.shape
    k, n = w.shape[-2:]
    assert k_pad % tk == 0 and k_pad - k < tk and n % tn == 0
    tm = _div_tile(m, tm_target, 16)
    nl = len(lead)
    return pl.pallas_call(
        functools.partial(_down_kernel, tk=tk, k_valid=None if k_pad == k else k),
        grid=(m // tm, n // tn, k_pad // tk),
        in_specs=[pl.BlockSpec((tm, tk), lambda i, j, kk: (i, kk)),
                  pl.BlockSpec((None,) * nl + (tk, tn), lambda i, j, kk: lead + (kk, j))],
        out_specs=pl.BlockSpec((tm, tn), lambda i, j, kk: (i, j)),
        out_shape=jax.ShapeDtypeStruct((m, n), F32),
        compiler_params=_params("parallel", "parallel", "arbitrary"),
        name="ffn_down",
    )(a, w)


def _select_by_group(g, values):
    out = values[-1]
    for idx in range(len(values) - 2, -1, -1):
        out = jnp.where(g == idx, values[idx], out)
    return out


def _pool_prompt_kernel(a_ref, halo_ref, w_ref, scale_ref, o_ref, *, tt):
    g, i = pl.program_id(0), pl.program_id(2)
    cur = a_ref[...]
    halo = jnp.where(i > 0, halo_ref[...], 0.0)
    s = jnp.concatenate([halo, cur], axis=0)
    sums, shift = [], 1
    for _ in POOL_WINDOWS:
        s = s + pltpu.roll(s, shift, 0)
        sums.append(s[POOL_HALO:, :])
        shift *= 2
    win = _select_by_group(g, sums)
    wsize = _select_by_group(g, [jnp.int32(w) for w in POOL_WINDOWS])
    pos = i * tt + lax.broadcasted_iota(jnp.int32, (tt, 1), 0)
    inv = 1.0 / jnp.minimum(pos + 1, wsize).astype(F32)
    d = win * inv - cur
    y = jnp.dot(d.astype(BF16), w_ref[...].astype(BF16), preferred_element_type=F32)
    o_ref[...] = (y * scale_ref[...]).astype(o_ref.dtype)


def _pool_prompt(proj, w_pool, pool_scale, *, lead, m_total, n_seq, seq, pw):
    n_g = len(POOL_WINDOWS)
    gw = pw // n_g
    tt = _div_tile(seq, 256, POOL_HALO)
    n_t = seq // tt
    hb = tt // POOL_HALO
    return pl.pallas_call(
        functools.partial(_pool_prompt_kernel, tt=tt),
        grid=(n_g, n_seq, n_t),
        in_specs=[pl.BlockSpec((tt, gw), lambda g, b, i: (b * n_t + i, g)),
                  pl.BlockSpec((POOL_HALO, gw), lambda g, b, i: (jnp.maximum((b * n_t + i) * hb - 1, 0), g)),
                  pl.BlockSpec((None,) * len(lead) + (None, gw, gw), lambda g, b, i: lead + (g, 0, 0)),
                  pl.BlockSpec((1, gw), lambda g, b, i: (lead[0], g))],
        out_specs=pl.BlockSpec((tt, gw), lambda g, b, i: (b * n_t + i, g)),
        out_shape=jax.ShapeDtypeStruct((m_total, pw), BF16),
        compiler_params=_params("parallel", "parallel", "parallel"),
        name="pool_prompt",
    )(proj, proj, w_pool, pool_scale)


def _pool_sample_kernel(a_ref, st_ref, w_ref, scale_ref, ya_hbm_ref, o_ref, *, past):
    del ya_hbm_ref
    g = pl.program_id(0)
    new = a_ref[...]
    n_hist = st_ref.shape[0]
    acc, sums = new, []
    for r in range(n_hist - 1, -1, -1):
        acc = acc + st_ref[r]
        if n_hist - r + 1 in POOL_WINDOWS:
            sums.append(acc)
    win = _select_by_group(g, sums)
    cnt = _select_by_group(g, [jnp.float32(min(past + 1, w)) for w in POOL_WINDOWS])
    d = win * (1.0 / cnt) - new
    y = jnp.dot(d.astype(BF16), w_ref[...].astype(BF16), preferred_element_type=F32)
    o_ref[...] = (y * scale_ref[...]).astype(o_ref.dtype)


def _pool_sample(proj, state_t, w_pool, pool_scale, ya, *, lead, m_prompt, nb, pw, past):
    n_g = len(POOL_WINDOWS)
    gw = pw // n_g
    rb = m_prompt // nb
    n_hist = state_t.shape[0]
    return pl.pallas_call(
        functools.partial(_pool_sample_kernel, past=past),
        grid=(n_g,),
        in_specs=[pl.BlockSpec((nb, gw), lambda g: (rb, g)),
                  pl.BlockSpec((n_hist, nb, gw), lambda g: (0, 0, g)),
                  pl.BlockSpec((None,) * len(lead) + (None, gw, gw), lambda g: lead + (g, 0, 0)),
                  pl.BlockSpec((1, gw), lambda g: (lead[0], g)),
                  pl.BlockSpec(memory_space=pl.ANY)],
        out_specs=pl.BlockSpec((nb, gw), lambda g: (rb, g)),
        out_shape=jax.ShapeDtypeStruct(ya.shape, ya.dtype),
        input_output_aliases={4: 0},
        compiler_params=_params("parallel"),
        name="pool_sample",
    )(proj, state_t, w_pool, pool_scale, ya)


def _sconv_prompt_kernel(b_ref, c_ref, v_ref, ch_ref, vh_ref, w_ref, o_ref, st_ref, *, tt, ksize):
    i = pl.program_id(2)
    z = c_ref[...] * v_ref[...]
    zh = jnp.where(i > 0, ch_ref[...] * vh_ref[...], 0.0)
    ext = jnp.concatenate([zh, z], axis=0)
    conv = w_ref[ksize - 1:ksize, :] * z
    for k in range(ksize - 1):
        shifted = pltpu.roll(ext, ksize - 1 - k, 0)[SCONV_HALO:, :]
        conv = conv + w_ref[k:k + 1, :] * shifted
    o_ref[...] = (b_ref[...] * conv).astype(o_ref.dtype)

    @pl.when(i == pl.num_programs(2) - 1)
    def _():
        st_ref[...] = z[tt - SCONV_HALO:, :]


def _sconv_prompt(proj, sconv_w, *, lead, m_total, n_seq, seq, pw, sw):
    ksize = sconv_w.shape[-2]
    cb = min(512, sw)
    tt = _div_tile(seq, 256, SCONV_HALO)
    n_t = seq // tt
    hb = tt // SCONV_HALO
    ob, oc, ov = pw // cb, (pw + sw) // cb, (pw + 2 * sw) // cb

    def cur(off):
        return pl.BlockSpec((tt, cb), lambda b, j, i: (b * n_t + i, off + j))

    def halo(off):
        return pl.BlockSpec((SCONV_HALO, cb), lambda b, j, i: (jnp.maximum((b * n_t + i) * hb - 1, 0), off + j))

    return pl.pallas_call(
        functools.partial(_sconv_prompt_kernel, tt=tt, ksize=ksize),
        grid=(n_seq, sw // cb, n_t),
        in_specs=[cur(ob), cur(oc), cur(ov), halo(oc), halo(ov),
                  pl.BlockSpec((None,) * len(lead) + (ksize, cb), lambda b, j, i: lead + (0, j))],
        out_specs=[pl.BlockSpec((tt, cb), lambda b, j, i: (b * n_t + i, j)),
                   pl.BlockSpec((None, SCONV_HALO, cb), lambda b, j, i: (b, 0, j))],
        out_shape=[jax.ShapeDtypeStruct((m_total, sw), BF16),
                   jax.ShapeDtypeStruct((n_seq, SCONV_HALO, sw), F32)],
        compiler_params=_params("parallel", "parallel", "arbitrary"),
        name="sconv_prompt",
    )(proj, proj, proj, proj, proj, sconv_w)


def _sconv_sample_kernel(b_ref, c_ref, v_ref, st_ref, w_ref, yb_hbm_ref, o_ref, z_ref, *, ksize):
    del yb_hbm_ref
    z = c_ref[...] * v_ref[...]
    conv = w_ref[ksize - 1:ksize, :] * z
    for k in range(ksize - 1):
        conv = conv + w_ref[k:k + 1, :] * st_ref[k]
    o_ref[...] = (b_ref[...] * conv).astype(o_ref.dtype)
    z_ref[...] = z


def _sconv_sample(proj, state_t, sconv_w, yb, *, lead, m_prompt, nb, pw, sw):
    ksize = sconv_w.shape[-2]
    cb = min(512, sw)
    rb = m_prompt // nb
    ob, oc, ov = pw // cb, (pw + sw) // cb, (pw + 2 * sw) // cb
    return pl.pallas_call(
        functools.partial(_sconv_sample_kernel, ksize=ksize),
        grid=(sw // cb,),
        in_specs=[pl.BlockSpec((nb, cb), lambda j: (rb, ob + j)),
                  pl.BlockSpec((nb, cb), lambda j: (rb, oc + j)),
                  pl.BlockSpec((nb, cb), lambda j: (rb, ov + j)),
                  pl.BlockSpec((ksize - 1, nb, cb), lambda j: (0, 0, j)),
                  pl.BlockSpec((None,) * len(lead) + (ksize, cb), lambda j: lead + (0, j)),
                  pl.BlockSpec(memory_space=pl.ANY)],
        out_specs=[pl.BlockSpec((nb, cb), lambda j: (rb, j)),
                   pl.BlockSpec((nb, cb), lambda j: (0, j))],
        out_shape=[jax.ShapeDtypeStruct(yb.shape, yb.dtype), jax.ShapeDtypeStruct((nb, sw), F32)],
        input_output_aliases={5: 0},
        compiler_params=_params("parallel"),
        name="sconv_sample",
    )(proj, proj, proj, state_t, sconv_w, yb)


def _ln_silu(cv, g, b):
    mu = jnp.mean(cv, axis=-1, keepdims=True)
    var = jnp.mean(jnp.square(cv - mu), axis=-1, keepdims=True)
    return _silu((cv - mu) * lax.rsqrt(var + EPS) * g + b)


def _conf_prompt_kernel(ga_ref, gg_ref, gah_ref, ggh_ref, w_ref, b_ref, lng_ref, lnb_ref, o_ref, st_ref,
                        *, tt, ksize):
    i = pl.program_id(1)
    glu = ga_ref[...] * jax.nn.sigmoid(gg_ref[...])
    gh = jnp.where(i > 0, gah_ref[...] * jax.nn.sigmoid(ggh_ref[...]), 0.0)
    ext = jnp.concatenate([gh, glu], axis=0)
    acc = jnp.zeros_like(glu) + b_ref[...]
    for s in range(SUBLANES):
        es = ext if s == 0 else pltpu.roll(ext, s, 0)
        for q in range(CONF_HALO // SUBLANES):
            shift = SUBLANES * q + s
            if shift <= ksize - 1:
                k = ksize - 1 - shift
                lo = CONF_HALO - SUBLANES * q
                acc = acc + w_ref[k:k + 1, :] * es[lo:lo + tt, :]
    o_ref[...] = _ln_silu(acc, lng_ref[...], lnb_ref[...]).astype(o_ref.dtype)

    @pl.when(i == pl.num_programs(1) - 1)
    def _():
        st_ref[...] = glu[tt - CONF_HALO:, :]


def _conf_prompt(proj, conf_w, conf_b, ln_g, ln_b, *, lead, m_total, n_seq, seq, cw):
    ksize = conf_w.shape[-2]
    assert ksize - 1 <= CONF_HALO - 1
    tt = _div_tile(seq, 128, CONF_HALO)
    n_t = seq // tt
    hb = tt // CONF_HALO
    nl = len(lead)

    def halo(off):
        return pl.BlockSpec((CONF_HALO, cw), lambda b, i: (jnp.maximum((b * n_t + i) * hb - 1, 0), off))

    vec = pl.BlockSpec((None,) * nl + (1, cw), lambda b, i: lead + (0, 0))
    return pl.pallas_call(
        functools.partial(_conf_prompt_kernel, tt=tt, ksize=ksize),
        grid=(n_seq, n_t),
        in_specs=[pl.BlockSpec((tt, cw), lambda b, i: (b * n_t + i, 0)),
                  pl.BlockSpec((tt, cw), lambda b, i: (b * n_t + i, 1)),
                  halo(0), halo(1),
                  pl.BlockSpec((None,) * nl + (ksize, cw), lambda b, i: lead + (0, 0)),
                  vec, vec, vec],
        out_specs=[pl.BlockSpec((tt, cw), lambda b, i: (b * n_t + i, 0)),
                   pl.BlockSpec((None, CONF_HALO, cw), lambda b, i: (b, 0, 0))],
        out_shape=[jax.ShapeDtypeStruct((m_total, cw), BF16),
                   jax.ShapeDtypeStruct((n_seq, CONF_HALO, cw), F32)],
        compiler_params=_params("parallel", "arbitrary"),
        name="conf_prompt",
    )(proj, proj, proj, proj, conf_w,
      conf_b.reshape(conf_b.shape[:nl] + (1, cw)), ln_g.reshape(ln_g.shape[:nl] + (1, cw)),
      ln_b.reshape(ln_b.shape[:nl] + (1, cw)))


def _conf_sample_conv_kernel(ga_ref, gg_ref, st_ref, w_ref, b_ref, cv_ref, glu_ref, *, ksize):
    glu = ga_ref[...] * jax.nn.sigmoid(gg_ref[...])
    acc = w_ref[ksize - 1:ksize, :] * glu + b_ref[...]
    for k in range(ksize - 1):
        acc = acc + w_ref[k:k + 1, :] * st_ref[k]
    cv_ref[...] = acc
    glu_ref[...] = glu


def _conf_sample_norm_kernel(cv_ref, lng_ref, lnb_ref, c_hbm_ref, o_ref):
    del c_hbm_ref
    o_ref[...] = _ln_silu(cv_ref[...], lng_ref[...], lnb_ref[...]).astype(o_ref.dtype)


def _conf_sample(proj, state_t, conf_w, conf_b, ln_g, ln_b, c_out, *, lead, m_prompt, nb, cw):
    ksize = conf_w.shape[-2]
    cb = min(512, cw)
    rb = m_prompt // nb
    nl = len(lead)
    cv, glu = pl.pallas_call(
        functools.partial(_conf_sample_conv_kernel, ksize=ksize),
        grid=(cw // cb,),
        in_specs=[pl.BlockSpec((nb, cb), lambda j: (rb, j)),
                  pl.BlockSpec((nb, cb), lambda j: (rb, cw // cb + j)),
                  pl.BlockSpec((ksize - 1, nb, cb), lambda j: (0, 0, j)),
                  pl.BlockSpec((None,) * nl + (ksize, cb), lambda j: lead + (0, j)),
                  pl.BlockSpec((None,) * nl + (1, cb), lambda j: lead + (0, j))],
        out_specs=[pl.BlockSpec((nb, cb), lambda j: (0, j)), pl.BlockSpec((nb, cb), lambda j: (0, j))],
        out_shape=[jax.ShapeDtypeStruct((nb, cw), F32), jax.ShapeDtypeStruct((nb, cw), F32)],
        compiler_params=_params("parallel"),
        name="conf_sample_conv",
    )(proj, proj, state_t, conf_w, conf_b.reshape(conf_b.shape[:nl] + (1, cw)))
    vec = pl.BlockSpec((None,) * nl + (1, cw), lambda j: lead + (0, 0))
    c_out = pl.pallas_call(
        _conf_sample_norm_kernel,
        grid=(1,),
        in_specs=[pl.BlockSpec((nb, cw), lambda j: (0, 0)), vec, vec, pl.BlockSpec(memory_space=pl.ANY)],
        out_specs=pl.BlockSpec((nb, cw), lambda j: (rb, 0)),
        out_shape=jax.ShapeDtypeStruct(c_out.shape, c_out.dtype),
        input_output_aliases={3: 0},
        compiler_params=_params("arbitrary"),
        name="conf_sample_norm",
    )(cv, ln_g.reshape(ln_g.shape[:nl] + (1, cw)), ln_b.reshape(ln_b.shape[:nl] + (1, cw)), c_out)
    return c_out, glu


def _rope_tables(pos, rope_dim):
    half = rope_dim // 2
    inv = ROPE_THETA ** (-jnp.arange(half, dtype=F32) * (2.0 / rope_dim))
    ang = pos.astype(F32)[:, None] * inv[None, :]
    cos, sin = jnp.cos(ang), jnp.sin(ang)
    zeros = jnp.zeros((pos.shape[0], LANES - rope_dim), F32)
    zh = jnp.zeros_like(sin)
    t_cos = jnp.concatenate([cos, cos, zeros], axis=1)
    t_up = jnp.concatenate([zh, sin, zeros], axis=1)
    t_down = jnp.concatenate([-sin, zh, zeros], axis=1)
    return t_cos, t_up, t_down


def _rope_lanes(v, t_cos, t_up, t_down, half):
    return v * t_cos + pltpu.roll(v, half, 1) * t_up + pltpu.roll(v, LANES - half, 1) * t_down


def _qproj_kernel(a_ref, w_ref, tc_ref, tu_ref, td_ref, o_ref, *, heads_per_tile, half):
    d = jnp.dot(a_ref[...], w_ref[...].astype(BF16), preferred_element_type=F32)
    tc, tu, td = tc_ref[...], tu_ref[...], td_ref[...]
    pieces = []
    for h in range(heads_per_tile):
        base = 2 * LANES * h
        pieces.append(d[:, base:base + LANES])
        pieces.append(_rope_lanes(d[:, base + LANES:base + 2 * LANES], tc, tu, td, half))
    o_ref[...] = jnp.concatenate(pieces, axis=1).astype(o_ref.dtype)


def _qproj(qn, w_uq_pad, tables, *, half):
    m, k = qn.shape
    n = w_uq_pad.shape[1]
    tn = min(512, n)
    tm = _div_tile(m, 1040, 16)
    tab = pl.BlockSpec((tm, LANES), lambda i, j: (i, 0))
    return pl.pallas_call(
        functools.partial(_qproj_kernel, heads_per_tile=tn // (2 * LANES), half=half),
        grid=(m // tm, n // tn),
        in_specs=[pl.BlockSpec((tm, k), lambda i, j: (i, 0)),
                  pl.BlockSpec((k, tn), lambda i, j: (0, j)), tab, tab, tab],
        out_specs=pl.BlockSpec((tm, tn), lambda i, j: (i, j)),
        out_shape=jax.ShapeDtypeStruct((m, n), BF16),
        compiler_params=_params("parallel", "arbitrary"),
        name="q_proj_rope",
    )(qn, w_uq_pad, *tables)


def _kv_kernel(kva_ref, kpe_ref, g_ref, tc_ref, tu_ref, td_ref, ckv_ref, ckvb_ref, kpe_o_ref, kpeb_ref, *, half):
    ckv = _rms(kva_ref[...], g_ref[...])
    ckv_ref[...] = ckv
    ckvb_ref[...] = ckv.astype(BF16)
    kpe = _rope_lanes(kpe_ref[...], tc_ref[...], tu_ref[...], td_ref[...], half)
    kpe_o_ref[...] = kpe
    kpeb_ref[...] = kpe.astype(BF16)


def _kv_finish(proj, kpe_raw, kv_norm_g, tables, *, kvl, col_block, half):
    m = proj.shape[0]
    tr = _div_tile(m, 256, 16)
    tab = pl.BlockSpec((tr, LANES), lambda i: (i, 0))
    lat = pl.BlockSpec((tr, kvl), lambda i: (i, 0))
    return pl.pallas_call(
        functools.partial(_kv_kernel, half=half),
        grid=(m // tr,),
        in_specs=[pl.BlockSpec((tr, kvl), lambda i: (i, col_block)), tab,
                  pl.BlockSpec((1, kvl), lambda i: (0, 0)), tab, tab, tab],
        out_specs=[lat, lat, tab, tab],
        out_shape=[jax.ShapeDtypeStruct((m, kvl), F32), jax.ShapeDtypeStruct((m, kvl), BF16),
                   jax.ShapeDtypeStruct((m, LANES), F32), jax.ShapeDtypeStruct((m, LANES), BF16)],
        compiler_params=_params("parallel"),
        name="kv_finish",
    )(proj, kpe_raw, kv_norm_g.reshape(1, kvl), *tables)


def _kproj_kernel(a_ref, w_ref, kpe_ref, o_ref, *, heads_per_tile):
    d = jnp.dot(a_ref[...], w_ref[...].astype(BF16), preferred_element_type=F32).astype(o_ref.dtype)
    kpe = kpe_ref[...]
    pieces = []
    for h in range(heads_per_tile):
        pieces.append(d[:, 2 * LANES * h:2 * LANES * h + LANES])
        pieces.append(kpe)
    o_ref[...] = jnp.concatenate(pieces, axis=1)


def _kproj(ckv_b, w_uk_pad, kpe_b, *, m_rows):
    k = ckv_b.shape[1]
    n = w_uk_pad.shape[1]
    tn = min(512, n)
    tm = _div_tile(m_rows, 1024, 16)
    return pl.pallas_call(
        functools.partial(_kproj_kernel, heads_per_tile=tn // (2 * LANES)),
        grid=(m_rows // tm, n // tn),
        in_specs=[pl.BlockSpec((tm, k), lambda i, j: (i, 0)),
                  pl.BlockSpec((k, tn), lambda i, j: (0, j)),
                  pl.BlockSpec((tm, LANES), lambda i, j: (i, 0))],
        out_specs=pl.BlockSpec((tm, tn), lambda i, j: (i, j)),
        out_shape=jax.ShapeDtypeStruct((m_rows, n), BF16),
        compiler_params=_params("parallel", "arbitrary"),
        name="k_proj",
    )(ckv_b, w_uk_pad, kpe_b)


def _flash_kernel(q_ref, k_ref, v_ref, o_ref, m_ref, l_ref, acc_ref, *, hg, tq, tk, vd, scale):
    qi, ki = pl.program_id(2), pl.program_id(3)

    @pl.when(ki == 0)
    def _():
        m_ref[...] = jnp.full(m_ref.shape, -jnp.inf, F32)
        l_ref[...] = jnp.zeros(l_ref.shape, F32)
        acc_ref[...] = jnp.zeros(acc_ref.shape, F32)

    @pl.when(ki * tk <= qi * tq + tq - 1)
    def _():
        qpos = qi * tq + lax.broadcasted_iota(jnp.int32, (tq, tk), 0)
        kpos = ki * tk + lax.broadcasted_iota(jnp.int32, (tq, tk), 1)
        mask = kpos <= qpos
        for h in range(hg):
            q = q_ref[:, 2 * LANES * h:2 * LANES * (h + 1)]
            k = k_ref[:, 2 * LANES * h:2 * LANES * (h + 1)]
            s = lax.dot_general(q, k, (((1,), (1,)), ((), ())), preferred_element_type=F32)
            s = jnp.where(mask, s * scale, NEG_INF)
            m_prev = m_ref[h]
            m_new = jnp.maximum(m_prev, jnp.max(s, axis=-1, keepdims=True))
            alpha = jnp.exp(m_prev - m_new)
            p = jnp.exp(s - m_new)
            l_ref[h] = alpha * l_ref[h] + jnp.sum(p, axis=-1, keepdims=True)
            pv = jnp.dot(p.astype(BF16), v_ref[:, vd * h:vd * (h + 1)], preferred_element_type=F32)
            acc_ref[h] = alpha * acc_ref[h] + pv
            m_ref[h] = m_new

    @pl.when(ki == pl.num_programs(3) - 1)
    def _():
        for h in range(hg):
            o_ref[:, vd * h:vd * (h + 1)] = (acc_ref[h] / l_ref[h]).astype(o_ref.dtype)


def _prompt_attention(q_cat, k_cat, v, *, m_total, n_seq, seq, n_heads, vd, scale):
    hg = math.gcd(n_heads, 4)
    tq = tk = _div_tile(seq, 512, 128)
    n_q, n_k = seq // tq, seq // tk

    def last_needed(qi):
        return (qi * tq + tq - 1) // tk

    return pl.pallas_call(
        functools.partial(_flash_kernel, hg=hg, tq=tq, tk=tk, vd=vd, scale=scale),
        grid=(n_seq, n_heads // hg, n_q, n_k),
        in_specs=[pl.BlockSpec((tq, hg * 2 * LANES), lambda b, g, qi, ki: (b * n_q + qi, g)),
                  pl.BlockSpec((tk, hg * 2 * LANES),
                               lambda b, g, qi, ki: (b * n_k + jnp.minimum(ki, last_needed(qi)), g)),
                  pl.BlockSpec((tk, hg * vd),
                               lambda b, g, qi, ki: (b * n_k + jnp.minimum(ki, last_needed(qi)), g))],
        out_specs=pl.BlockSpec((tq, hg * vd), lambda b, g, qi, ki: (b * n_q + qi, g)),
        out_shape=jax.ShapeDtypeStruct((m_total, n_heads * vd), BF16),
        scratch_shapes=[pltpu.VMEM((hg, tq, 1), F32), pltpu.VMEM((hg, tq, 1), F32),
                        pltpu.VMEM((hg, tq, vd), F32)],
        compiler_params=_params("parallel", "parallel", "parallel", "arbitrary"),
        name="prompt_attention",
    )(q_cat, k_cat, v)


def _qabsorb_kernel(q_ref, w_ref, o_ref):
    o_ref[...] = lax.dot_general(q_ref[...], w_ref[...].astype(BF16), (((1,), (1,)), ((), ())),
                                 preferred_element_type=F32).astype(o_ref.dtype)


def _qabsorb(q_cat, w_uk2, *, m_prompt, nb, n_heads, nope):
    kvl = w_uk2.shape[0]
    rb = m_prompt // nb
    return pl.pallas_call(
        _qabsorb_kernel,
        grid=(n_heads,),
        in_specs=[pl.BlockSpec((nb, nope), lambda h: (rb, 2 * h)),
                  pl.BlockSpec((kvl, nope), lambda h: (0, h))],
        out_specs=pl.BlockSpec((nb, kvl), lambda h: (0, h)),
        out_shape=jax.ShapeDtypeStruct((nb, n_heads * kvl), BF16),
        compiler_params=_params("parallel"),
        name="q_absorb",
    )(q_cat, w_uk2)


def _paged_kernel(pt_ref, qlat_ref, qpe_ref, cn_ref, pn_ref, *rest, n_pp, ps, rope_dim, scale):
    del pt_ref
    ckv_refs, kpe_refs = rest[:n_pp], rest[n_pp:2 * n_pp]
    o_ref, keys_ref, s_ref = rest[2 * n_pp:]
    c = pl.program_id(1)
    n_chunks = s_ref.shape[0]
    qlat = qlat_ref[...]
    qpe = qpe_ref[...]

    scores = []
    for r in range(n_pp):
        kc = ckv_refs[r][...].astype(BF16)
        kp = kpe_refs[r][...].astype(BF16)
        keys_ref[c * n_pp + r] = kc
        s = lax.dot_general(qlat, kc, (((1,), (1,)), ((), ())), preferred_element_type=F32)
        s = s + lax.dot_general(qpe[:, :rope_dim], kp, (((1,), (1,)), ((), ())), preferred_element_type=F32)
        scores.append(s)
    s_ref[c] = jnp.concatenate(scores, axis=1) * scale

    @pl.when(c == n_chunks - 1)
    def _():
        cn = cn_ref[...].astype(F32)
        pn = pn_ref[...].astype(F32)
        s_new = (jnp.sum(qlat.astype(F32) * cn, axis=-1, keepdims=True)
                 + jnp.sum(qpe.astype(F32) * pn, axis=-1, keepdims=True)) * scale
        m = s_new
        for i in range(n_chunks):
            m = jnp.maximum(m, jnp.max(s_ref[i], axis=-1, keepdims=True))
        e_new = jnp.exp(s_new - m)
        total = e_new
        for i in range(n_chunks):
            total = total + jnp.sum(jnp.exp(s_ref[i] - m), axis=-1, keepdims=True)
        acc = (e_new / total).astype(BF16).astype(F32) * cn
        for i in range(n_chunks):
            p = (jnp.exp(s_ref[i] - m) / total).astype(BF16)
            k_chunk = keys_ref[i * n_pp:(i + 1) * n_pp].reshape(n_pp * ps, keys_ref.shape[-1])
            acc = acc + jnp.dot(p, k_chunk, preferred_element_type=F32)
        o_ref[...] = acc.astype(o_ref.dtype)


def _sample_attention(q_lat, q_pe, ckv_new, kpe_new, cache_ckv, cache_kpe, page_table, *, layer, scale):
    nb, n_heads, kvl = q_lat.shape
    n_pages = page_table.shape[1]
    ps = cache_ckv.shape[2]
    rope_dim = cache_kpe.shape[3]
    n_pp = math.gcd(n_pages, 8)

    def page_spec(width, r):
        return pl.BlockSpec((None, None, ps, width),
                            lambda n, c, pt: (layer, pt[n * n_pages + c * n_pp + r], 0, 0))

    in_specs = [pl.BlockSpec((None, n_heads, kvl), lambda n, c, pt: (n, 0, 0)),
                pl.BlockSpec((None, n_heads, LANES), lambda n, c, pt: (n, 0, 0)),
                pl.BlockSpec((None, 1, kvl), lambda n, c, pt: (n, 0, 0)),
                pl.BlockSpec((None, 1, LANES), lambda n, c, pt: (n, 0, 0))]
    in_specs += [page_spec(kvl, r) for r in range(n_pp)]
    in_specs += [page_spec(rope_dim, r) for r in range(n_pp)]
    grid_spec = pltpu.PrefetchScalarGridSpec(
        num_scalar_prefetch=1,
        grid=(nb, n_pages // n_pp),
        in_specs=in_specs,
        out_specs=pl.BlockSpec((None, n_heads, kvl), lambda n, c, pt: (n, 0, 0)),
        scratch_shapes=[pltpu.VMEM((n_pages, ps, kvl), BF16),
                        pltpu.VMEM((n_pages // n_pp, n_heads, n_pp * ps), F32)],
    )
    return pl.pallas_call(
        functools.partial(_paged_kernel, n_pp=n_pp, ps=ps, rope_dim=rope_dim, scale=scale),
        grid_spec=grid_spec,
        out_shape=jax.ShapeDtypeStruct((nb, n_heads, kvl), BF16),
        compiler_params=_params("parallel", "arbitrary"),
        name="sample_attention",
    )(page_table.reshape(-1), q_lat, q_pe, ckv_new, kpe_new,
      *([cache_ckv] * n_pp), *([cache_kpe] * n_pp))


def _vup_kernel(o_ref_in, w_ref, attn_hbm_ref, o_ref):
    del attn_hbm_ref
    o_ref[...] = jnp.dot(o_ref_in[...], w_ref[...].astype(BF16), preferred_element_type=F32).astype(o_ref.dtype)


def _sample_vup(o_lat2, w_uv2, attn, *, m_prompt, nb, n_heads, vd):
    kvl = w_uv2.shape[0]
    rb = m_prompt // nb
    return pl.pallas_call(
        _vup_kernel,
        grid=(n_heads,),
        in_specs=[pl.BlockSpec((nb, kvl), lambda h: (0, h)),
                  pl.BlockSpec((kvl, vd), lambda h: (0, h)),
                  pl.BlockSpec(memory_space=pl.ANY)],
        out_specs=pl.BlockSpec((nb, vd), lambda h: (rb, h)),
        out_shape=jax.ShapeDtypeStruct(attn.shape, attn.dtype),
        input_output_aliases={2: 0},
        compiler_params=_params("parallel"),
        name="sample_v_up",
    )(o_lat2, w_uv2, attn)


def _router_kernel(h_ref, w_ref, b_ref, idx_ref, gate_ref, *, n_experts):
    logits = jnp.dot(h_ref[...].astype(BF16), w_ref[...].astype(BF16), preferred_element_type=F32) + b_ref[...]
    lane = lax.broadcasted_iota(jnp.int32, logits.shape, 1)
    logits = jnp.where(lane < n_experts, logits, -jnp.inf)
    m1 = jnp.max(logits, axis=-1, keepdims=True)
    i1 = jnp.min(jnp.where(logits == m1, lane, LANES), axis=-1, keepdims=True)
    rest = jnp.where(lane == i1, -jnp.inf, logits)
    m2 = jnp.max(rest, axis=-1, keepdims=True)
    i2 = jnp.min(jnp.where(rest == m2, lane, LANES), axis=-1, keepdims=True)
    e2 = jnp.exp(m2 - m1)
    denom = 1.0 + e2
    idx_ref[...] = jnp.where(lane == 0, i1, jnp.where(lane == 1, i2, 0))
    gate_ref[...] = jnp.where(lane == 0, 1.0 / denom, jnp.where(lane == 1, e2 / denom, 0.0))


def _router(h, router_w_pad, router_b_pad, *, n_experts):
    m, d = h.shape
    tr = _div_tile(m, 256, 8)
    row = pl.BlockSpec((tr, LANES), lambda i: (i, 0))
    return pl.pallas_call(
        functools.partial(_router_kernel, n_experts=n_experts),
        grid=(m // tr,),
        in_specs=[pl.BlockSpec((tr, d), lambda i: (i, 0)),
                  pl.BlockSpec((d, LANES), lambda i: (0, 0)),
                  pl.BlockSpec((1, LANES), lambda i: (0, 0))],
        out_specs=[row, row],
        out_shape=[jax.ShapeDtypeStruct((m, LANES), jnp.int32), jax.ShapeDtypeStruct((m, LANES), F32)],
        compiler_params=_params("parallel"),
        name="router_top2",
    )(h, router_w_pad, router_b_pad)


def _row_copy(src_hbm, row, dst, r, sem):
    return pltpu.make_async_copy(src_hbm.at[pl.ds(row, 1), :], dst.at[pl.ds(r, 1), :], sem)


def _gather_kernel(src_ref, nu_ref, h_hbm, o_ref, buf, sem, *, rows):
    i = pl.program_id(0)

    @pl.when(i < nu_ref[0])
    def _():
        base = i * rows

        def issue(r, carry):
            _row_copy(h_hbm, src_ref[base + r], buf, r, sem).start()
            return carry

        def drain(r, carry):
            _row_copy(h_hbm, 0, buf, r, sem).wait()
            return carry

        lax.fori_loop(0, rows, issue, 0)
        lax.fori_loop(0, rows, drain, 0)
        o_ref[...] = buf[...].astype(o_ref.dtype)


def _gather_rows(h, src_rows, n_chunks_used, *, rows):
    mp = src_rows.shape[0]
    d = h.shape[1]
    grid_spec = pltpu.PrefetchScalarGridSpec(
        num_scalar_prefetch=2,
        grid=(mp // rows,),
        in_specs=[pl.BlockSpec(memory_space=pl.ANY)],
        out_specs=pl.BlockSpec((rows, d), lambda i, src, nu: (jnp.minimum(i, nu[0] - 1), 0)),
        scratch_shapes=[pltpu.VMEM((rows, d), F32), pltpu.SemaphoreType.DMA(())],
    )
    return pl.pallas_call(
        functools.partial(_gather_kernel, rows=rows),
        grid_spec=grid_spec,
        out_shape=jax.ShapeDtypeStruct((mp, d), BF16),
        compiler_params=_params("arbitrary"),
        name="moe_gather",
    )(src_rows, n_chunks_used, h)


def _moe_gateup_kernel(te_ref, nu_ref, a_ref, w1_ref, w3_ref, o_ref):
    del te_ref

    @pl.when(pl.program_id(0) < nu_ref[0])
    def _():
        _gateup_body(a_ref[...], w1_ref, w3_ref, o_ref, 0, None)


def _moe_gateup(xs, w1, w3, tile_expert, n_used, *, layer, tm, tn):
    mp, d = xs.shape
    f = w1.shape[-1]
    assert f % tn == 0
    n_j = f // tn

    def ii(i, nu):
        return jnp.minimum(i, nu[0] - 1)

    def jj(i, j, nu):
        return jnp.where(i < nu[0], j, n_j - 1)

    wspec = pl.BlockSpec((None, None, d, tn), lambda i, j, te, nu: (layer, te[ii(i, nu)], 0, jj(i, j, nu)))
    grid_spec = pltpu.PrefetchScalarGridSpec(
        num_scalar_prefetch=2,
        grid=(mp // tm, n_j),
        in_specs=[pl.BlockSpec((tm, d), lambda i, j, te, nu: (ii(i, nu), 0)), wspec, wspec],
        out_specs=pl.BlockSpec((tm, tn), lambda i, j, te, nu: (ii(i, nu), jj(i, j, nu))),
    )
    return pl.pallas_call(
        _moe_gateup_kernel,
        grid_spec=grid_spec,
        out_shape=jax.ShapeDtypeStruct((mp, f), BF16),
        compiler_params=_params("arbitrary", "arbitrary"),
        name="moe_gateup",
    )(tile_expert, n_used, xs, w1, w3)


def _moe_down_kernel(te_ref, nu_ref, a_ref, w_ref, o_ref, *, tk):
    del te_ref

    @pl.when(pl.program_id(0) < nu_ref[0])
    def _():
        _down_body(a_ref, w_ref, o_ref, pl.program_id(2), tk, None)


def _moe_down(act, w2, tile_expert, n_used, *, layer, tm, tn, tk):
    mp, f = act.shape
    d = w2.shape[-1]
    assert f % tk == 0 and d % tn == 0
    n_j, n_k = d // tn, f // tk

    def ii(i, nu):
        return jnp.minimum(i, nu[0] - 1)

    def jj(i, j, nu):
        return jnp.where(i < nu[0], j, n_j - 1)

    def kk(i, k, nu):
        return jnp.where(i < nu[0], k, n_k - 1)

    grid_spec = pltpu.PrefetchScalarGridSpec(
        num_scalar_prefetch=2,
        grid=(mp // tm, n_j, n_k),
        in_specs=[pl.BlockSpec((tm, tk), lambda i, j, k, te, nu: (ii(i, nu), kk(i, k, nu))),
                  pl.BlockSpec((None, None, tk, tn),
                               lambda i, j, k, te, nu: (layer, te[ii(i, nu)], kk(i, k, nu), jj(i, j, nu)))],
        out_specs=pl.BlockSpec((tm, tn), lambda i, j, k, te, nu: (ii(i, nu), jj(i, j, nu))),
    )
    return pl.pallas_call(
        functools.partial(_moe_down_kernel, tk=tk),
        grid_spec=grid_spec,
        out_shape=jax.ShapeDtypeStruct((mp, d), F32),
        compiler_params=_params("arbitrary", "arbitrary", "arbitrary"),
        name="moe_down",
    )(tile_expert, n_used, act, w2)


def _combine_kernel(dest_ref, o_hbm, gate_ref, x_ref, g_ref, xo_ref, buf0, buf1, sem, *, rows):
    base = pl.program_id(0) * rows

    def issue(r, carry):
        _row_copy(o_hbm, dest_ref[2 * (base + r)], buf0, r, sem).start()
        _row_copy(o_hbm, dest_ref[2 * (base + r) + 1], buf1, r, sem).start()
        return carry

    def drain(r, carry):
        _row_copy(o_hbm, 0, buf0, r, sem).wait()
        _row_copy(o_hbm, 0, buf1, r, sem).wait()
        return carry

    lax.fori_loop(0, rows, issue, 0)
    lax.fori_loop(0, rows, drain, 0)
    gates = gate_ref[...]
    y = gates[:, 0:1] * buf0[...] + gates[:, 1:2] * buf1[...]
    xo_ref[...] = x_ref[...] + _rms(y, g_ref[...])


def _moe_combine(o_sorted, dest, gates, x, g_post):
    m, d = x.shape
    rows = _div_tile(m, 128, 8)
    grid_spec = pltpu.PrefetchScalarGridSpec(
        num_scalar_prefetch=1,
        grid=(m // rows,),
        in_specs=[pl.BlockSpec(memory_space=pl.ANY),
                  pl.BlockSpec((rows, LANES), lambda i, dest: (i, 0)),
                  pl.BlockSpec((rows, d), lambda i, dest: (i, 0)),
                  pl.BlockSpec((1, d), lambda i, dest: (0, 0))],
        out_specs=pl.BlockSpec((rows, d), lambda i, dest: (i, 0)),
        scratch_shapes=[pltpu.VMEM((rows, d), F32), pltpu.VMEM((rows, d), F32), pltpu.SemaphoreType.DMA(())],
    )
    return pl.pallas_call(
        functools.partial(_combine_kernel, rows=rows),
        grid_spec=grid_spec,
        out_shape=jax.ShapeDtypeStruct((m, d), F32),
        compiler_params=_params("arbitrary"),
        name="moe_combine",
    )(dest, o_sorted, gates, x, g_post.reshape(1, d))


def _route_plan(idx, *, n_experts, tm, n_tiles_max):
    m = idx.shape[0]
    e_flat = idx[:, :2].reshape(-1)
    onehot = (e_flat[:, None] == jnp.arange(n_experts, dtype=jnp.int32)[None, :]).astype(jnp.int32)
    csum = jnp.cumsum(onehot, axis=0)
    rank = jnp.take_along_axis(csum, e_flat[:, None], axis=1)[:, 0] - 1
    counts = csum[-1]
    tiles = (counts + tm - 1) // tm
    tile_end = jnp.cumsum(tiles)
    row_start = (tile_end - tiles) * tm
    dest = (row_start[e_flat] + rank).astype(jnp.int32)
    n_used = tile_end[-1:].astype(jnp.int32)
    tile_expert = jnp.minimum(
        jnp.searchsorted(tile_end, jnp.arange(n_tiles_max, dtype=jnp.int32), side="right"),
        n_experts - 1).astype(jnp.int32)
    src_rows = jnp.zeros((n_tiles_max * tm,), jnp.int32).at[dest].set(jnp.arange(2 * m, dtype=jnp.int32) // 2)
    return dest, src_rows, tile_expert, n_used


def kernel(x_prompt, x_sample, state_pool, state_sconv, state_conf, cache_ckv, cache_kpe, page_table, norm_mix_pre, norm_mix_post, norm_ffn_pre, norm_ffn_post, w_in0, w_pool, pool_scale, sconv_w, w_out0, w_in1, conf_w, conf_b, conf_ln_g, conf_ln_b, q_norm_g, w_uq, kv_norm_g, w_uk, w_uv, w_out1, ffn_w1, ffn_w3, ffn_w2, router_w, router_b, moe_w1, moe_w3, moe_w2):
    n_seq, seq, d = x_prompt.shape
    nb, t_s, _ = x_sample.shape
    assert t_s == 1 and norm_mix_pre.shape[0] == 2
    m_p = n_seq * seq
    m = m_p + nb
    assert m_p % nb == 0 and nb % 16 == 0
    pw = w_pool.shape[1] * w_pool.shape[2]
    sw = sconv_w.shape[-1]
    cw = conf_w.shape[-1]
    ql = q_norm_g.shape[-1]
    kvl = kv_norm_g.shape[-1]
    n_heads = w_uq.shape[2]
    rope_dim = cache_kpe.shape[-1]
    nope = w_uq.shape[3] - rope_dim
    vd = w_uv.shape[-1]
    n_experts = router_w.shape[-1]
    assert w_pool.shape[1] == len(POOL_WINDOWS) and state_pool.shape[2] == max(POOL_WINDOWS) - 1
    assert nope == LANES and vd == LANES and rope_dim <= LANES // 2
    n_pages, ps = page_table.shape[1], cache_ckv.shape[2]
    past = n_pages * ps
    scale = float(nope + rope_dim) ** -0.5
    half = rope_dim // 2

    x = jnp.concatenate([x_prompt.reshape(m_p, d), x_sample.reshape(nb, d)], axis=0)

    h = _rms_cast(x, norm_mix_pre[0])
    proj = _matmul([h], w_in0, lead=(0,), name="in_proj0")
    ya = _pool_prompt(proj, w_pool, pool_scale, lead=(0,), m_total=m, n_seq=n_seq, seq=seq, pw=pw)
    ya = _pool_sample(proj, jnp.swapaxes(state_pool[0], 0, 1), w_pool, pool_scale, ya,
                      lead=(0,), m_prompt=m_p, nb=nb, pw=pw, past=past)
    yb, z_tail = _sconv_prompt(proj, sconv_w, lead=(0,), m_total=m, n_seq=n_seq, seq=seq, pw=pw, sw=sw)
    yb, z_new = _sconv_sample(proj, jnp.swapaxes(state_sconv[0], 0, 1), sconv_w, yb,
                              lead=(0,), m_prompt=m_p, nb=nb, pw=pw, sw=sw)
    y = _matmul([ya, yb], w_out0, lead=(0,), name="out_proj0")
    x, h = _resid_norm(x, y, norm_mix_post[0], norm_ffn_pre[0], BF16)
    act = _gateup(h, ffn_w1, ffn_w3, lead=(0,), tn=256, pad_to=512, tm_target=1040)
    f = _down(act, ffn_w2, lead=(0,), tn=1024, tk=512, tm_target=2080)
    x, h = _resid_norm(x, f, norm_ffn_post[0], norm_mix_pre[1], BF16)

    n_hist_pool = state_pool.shape[2]
    k_sconv = sconv_w.shape[-2]
    a_p = proj[:m_p, :pw].reshape(n_seq, seq, pw)
    new_pool_p = a_p[:, seq - n_hist_pool:, :][None]
    new_pool_s = jnp.concatenate([state_pool[0][:, 1:, :], proj[m_p:, None, :pw]], axis=1)[None]
    new_sconv_p = z_tail[:, SCONV_HALO - (k_sconv - 1):, :][None]
    new_sconv_s = jnp.concatenate([state_sconv[0][:, 1:, :], z_new[:, None, :]], axis=1)[None]

    main_cols = 2 * cw + ql + kvl
    proj = _matmul([h], w_in1, lead=(0,), n_cols=main_cols, tn=_div_tile(main_cols, 512, LANES), name="in_proj1")
    w_kpe = jnp.pad(w_in1[0][:, main_cols:], ((0, 0), (0, LANES - rope_dim)))
    kpe_raw = _matmul([h], w_kpe, name="in_proj1_rope_key")

    c_out, glu_tail = _conf_prompt(proj, conf_w, conf_b, conf_ln_g, conf_ln_b,
                                   lead=(0,), m_total=m, n_seq=n_seq, seq=seq, cw=cw)
    c_out, glu_new = _conf_sample(proj, jnp.swapaxes(state_conf[0], 0, 1), conf_w, conf_b, conf_ln_g, conf_ln_b,
                                  c_out, lead=(0,), m_prompt=m_p, nb=nb, cw=cw)

    pos = jnp.concatenate([jnp.tile(jnp.arange(seq, dtype=jnp.int32), n_seq),
                           jnp.full((nb,), past, jnp.int32)])
    tables = _rope_tables(pos, rope_dim)
    qn = _rms_cast(proj, q_norm_g[0], width=ql, col_block=2 * cw // ql)
    w_uq_pad = jnp.pad(w_uq[0], ((0, 0), (0, 0), (0, 2 * LANES - nope - rope_dim))).reshape(ql, n_heads * 2 * LANES)
    q_cat = _qproj(qn, w_uq_pad, tables, half=half)
    ckv, ckv_b, kpe, kpe_b = _kv_finish(proj, kpe_raw, kv_norm_g[0], tables,
                                        kvl=kvl, col_block=(2 * cw + ql) // kvl, half=half)

    w_uk_pad = jnp.pad(w_uk[0], ((0, 0), (0, 0), (0, LANES))).reshape(kvl, n_heads * 2 * LANES)
    k_cat = _kproj(ckv_b, w_uk_pad, kpe_b, m_rows=m_p)
    w_uv2 = w_uv[0].reshape(kvl, n_heads * vd)
    v_p = _matmul([ckv_b[:m_p]], w_uv2, out_dtype=BF16, tm_target=1024, name="v_proj")
    attn = _prompt_attention(q_cat, k_cat, v_p, m_total=m, n_seq=n_seq, seq=seq, n_heads=n_heads, vd=vd, scale=scale)

    w_uk2 = w_uk[0].reshape(kvl, n_heads * nope)
    q_lat = _qabsorb(q_cat, w_uk2, m_prompt=m_p, nb=nb, n_heads=n_heads, nope=nope)
    q_pe_s = q_cat[m_p:].reshape(nb, n_heads, 2 * LANES)[:, :, LANES:]
    o_lat = _sample_attention(q_lat.reshape(nb, n_heads, kvl), q_pe_s, ckv_b[m_p:].reshape(nb, 1, kvl),
                              kpe_b[m_p:].reshape(nb, 1, LANES), cache_ckv, cache_kpe, page_table,
                              layer=0, scale=scale)
    attn = _sample_vup(o_lat.reshape(nb, n_heads * kvl), w_uv2, attn, m_prompt=m_p, nb=nb, n_heads=n_heads, vd=vd)

    y = _matmul([c_out, attn], w_out1, lead=(0,), tn=256, name="out_proj1")
    x, h32 = _resid_norm(x, y, norm_mix_post[1], norm_ffn_pre[1], F32)

    router_w_pad = jnp.pad(router_w[0], ((0, 0), (0, LANES - n_experts)))
    router_b_pad = jnp.pad(router_b[0], (0, LANES - n_experts)).reshape(1, LANES)
    idx, gates = _router(h32, router_w_pad, router_b_pad, n_experts=n_experts)

    tm_g = MOE_TILE_ROWS if 2 * m // n_experts >= MOE_TILE_ROWS else 64
    n_tiles_max = (2 * m + n_experts * (tm_g - 1) + tm_g - 1) // tm_g
    dest, src_rows, tile_expert, n_used = _route_plan(idx, n_experts=n_experts, tm=tm_g, n_tiles_max=n_tiles_max)
    gather_rows = _div_tile(tm_g, 128, 16)
    xs = _gather_rows(h32, src_rows, n_used * (tm_g // gather_rows), rows=gather_rows)
    f_e = moe_w1.shape[-1]
    act = _moe_gateup(xs, moe_w1, moe_w3, tile_expert, n_used, layer=0, tm=tm_g, tn=_div_tile(f_e, 256, LANES))
    o_sorted = _moe_down(act, moe_w2, tile_expert, n_used, layer=0, tm=tm_g,
                         tn=_div_tile(d, 2048, LANES), tk=_div_tile(f_e, 1024, LANES))
    x = _moe_combine(o_sorted, dest, gates, x, norm_ffn_post[1])

    k_conf = conf_w.shape[-2]
    new_conf_p = glu_tail[:, CONF_HALO - (k_conf - 1):, :][None]
    new_conf_s = jnp.concatenate([state_conf[0][:, 1:, :], glu_new[:, None, :]], axis=1)[None]
    new_ckv_p = ckv[:m_p].reshape(1, n_seq, seq, kvl)
    new_ckv_s = ckv[m_p:].reshape(1, nb, 1, kvl)
    new_kpe_p = kpe[:m_p, :rope_dim].reshape(1, n_seq, seq, rope_dim)
    new_kpe_s = kpe[m_p:, :rope_dim].reshape(1, nb, 1, rope_dim)

    return (x[:m_p].reshape(n_seq, seq, d), x[m_p:].reshape(nb, 1, d),
            new_pool_p, new_pool_s, new_sconv_p, new_sconv_s, new_conf_p, new_conf_s,
            new_ckv_p, new_ckv_s, new_kpe_p, new_kpe_s)
```

```python
import functools
import math

import jax
import jax.numpy as jnp
from jax import lax
from jax.experimental import pallas as pl
from jax.experimental.pallas import tpu as pltpu

F32 = jnp.float32
BF16 = jnp.bfloat16

EPS = 1e-6
ROPE_THETA = 10000.0
NEG_INF = -1e30
POOL_WINDOWS = (2, 4, 8, 16)

V7X_VMEM_BYTES = 64 * 1024 * 1024
LANES = 128
SUBLANES = 8
VMEM_LIMIT = V7X_VMEM_BYTES - 8 * 1024 * 1024

POOL_HALO = 16
SCONV_HALO = 8
CONF_HALO = 32
MOE_TILE_ROWS = 1088


def _params(*sem):
    return pltpu.CompilerParams(dimension_semantics=sem, vmem_limit_bytes=VMEM_LIMIT)


def _div_tile(n, target, align):
    best = None
    for t in range(align, min(n, target) + 1, align):
        if n % t == 0:
            best = t
    return n if best is None else best


def _rms(x, g):
    return x * lax.rsqrt(jnp.mean(x * x, axis=-1, keepdims=True) + EPS) * g


def _rms_cast_kernel(x_ref, g_ref, o_ref):
    o_ref[...] = _rms(x_ref[...], g_ref[...]).astype(o_ref.dtype)


def _rms_cast(x, g, *, width=None, col_block=0, out_dtype=BF16):
    m = x.shape[0]
    width = x.shape[1] if width is None else width
    tr = _div_tile(m, 256, 16)
    return pl.pallas_call(
        _rms_cast_kernel,
        grid=(m // tr,),
        in_specs=[pl.BlockSpec((tr, width), lambda i: (i, col_block)),
                  pl.BlockSpec((1, width), lambda i: (0, 0))],
        out_specs=pl.BlockSpec((tr, width), lambda i: (i, 0)),
        out_shape=jax.ShapeDtypeStruct((m, width), out_dtype),
        compiler_params=_params("parallel"),
        name="rms_cast",
    )(x, g.reshape(1, width))


def _resid_norm_kernel(x_ref, y_ref, gpost_ref, gpre_ref, xo_ref, h_ref):
    xn = x_ref[...] + _rms(y_ref[...], gpost_ref[...])
    xo_ref[...] = xn
    h_ref[...] = _rms(xn, gpre_ref[...]).astype(h_ref.dtype)


def _resid_norm(x, y, g_post, g_pre, h_dtype):
    m, d = x.shape
    tr = _div_tile(m, 256, 16)
    row = pl.BlockSpec((tr, d), lambda i: (i, 0))
    vec = pl.BlockSpec((1, d), lambda i: (0, 0))
    return pl.pallas_call(
        _resid_norm_kernel,
        grid=(m // tr,),
        in_specs=[row, row, vec, vec],
        out_specs=[row, row],
        out_shape=[jax.ShapeDtypeStruct((m, d), F32), jax.ShapeDtypeStruct((m, d), h_dtype)],
        compiler_params=_params("parallel"),
        name="resid_norm",
    )(x, y, g_post.reshape(1, d), g_pre.reshape(1, d))


def _mm_kernel(*refs, n_a):
    a_refs, w_ref, o_ref = refs[:n_a], refs[n_a], refs[n_a + 1]
    acc, off = None, 0
    for a_ref in a_refs:
        k = a_ref.shape[1]
        d = jnp.dot(a_ref[...], w_ref[off:off + k, :].astype(BF16), preferred_element_type=F32)
        acc = d if acc is None else acc + d
        off += k
    o_ref[...] = acc.astype(o_ref.dtype)


def _matmul(a_list, w, *, lead=(), n_cols=None, tn=512, tm_target=1040, out_dtype=F32, name="matmul"):
    m = a_list[0].shape[0]
    k_total = sum(a.shape[1] for a in a_list)
    assert w.shape[len(lead)] == k_total
    n = w.shape[-1] if n_cols is None else n_cols
    tn = min(tn, n)
    assert n % tn == 0
    tm = _div_tile(m, tm_target, 16)
    nl = len(lead)
    in_specs = [pl.BlockSpec((tm, a.shape[1]), lambda i, j: (i, 0)) for a in a_list]
    in_specs.append(pl.BlockSpec((None,) * nl + (k_total, tn), lambda i, j: lead + (0, j)))
    return pl.pallas_call(
        functools.partial(_mm_kernel, n_a=len(a_list)),
        grid=(m // tm, n // tn),
        in_specs=in_specs,
        out_specs=pl.BlockSpec((tm, tn), lambda i, j: (i, j)),
        out_shape=jax.ShapeDtypeStruct((m, n), out_dtype),
        compiler_params=_params("parallel", "arbitrary"),
        name=name,
    )(*a_list, w)


def _silu(x):
    return x * jax.nn.sigmoid(x)


def _gateup_body(a, w1_ref, w3_ref, o_ref, col0, n_valid):
    g = jnp.dot(a, w1_ref[...].astype(BF16), preferred_element_type=F32)
    u = jnp.dot(a, w3_ref[...].astype(BF16), preferred_element_type=F32)
    act = _silu(g) * u
    if n_valid is not None:
        col = col0 + lax.broadcasted_iota(jnp.int32, act.shape, 1)
        act = jnp.where(col < n_valid, act, 0.0)
    o_ref[...] = act.astype(o_ref.dtype)


def _gateup_kernel(a_ref, w1_ref, w3_ref, o_ref, *, tn, n_valid):
    _gateup_body(a_ref[...], w1_ref, w3_ref, o_ref, pl.program_id(1) * tn, n_valid)


def _gateup(a, w1, w3, *, lead, tn, pad_to, tm_target):
    m, k = a.shape
    n = w1.shape[-1]
    assert pad_to % tn == 0
    n_pad = pl.cdiv(n, pad_to) * pad_to
    n_blocks = n_pad // tn
    last_w_block = pl.cdiv(n, tn) - 1
    tm = _div_tile(m, tm_target, 16)
    nl = len(lead)
    wspec = pl.BlockSpec((None,) * nl + (k, tn), lambda i, j: lead + (0, jnp.minimum(j, last_w_block)))
    return pl.pallas_call(
        functools.partial(_gateup_kernel, tn=tn, n_valid=None if n_pad == n else n),
        grid=(m // tm, n_blocks),
        in_specs=[pl.BlockSpec((tm, k), lambda i, j: (i, 0)), wspec, wspec],
        out_specs=pl.BlockSpec((tm, tn), lambda i, j: (i, j)),
        out_shape=jax.ShapeDtypeStruct((m, n_pad), BF16),
        compiler_params=_params("parallel", "arbitrary"),
        name="ffn_gateup",
    )(a, w1, w3)


def _down_body(a_ref, w_ref, o_ref, k_step, tk, k_valid):
    w = w_ref[...]
    if k_valid is not None:
        row = k_step * tk + lax.broadcasted_iota(jnp.int32, w.shape, 0)
        w = jnp.where(row < k_valid, w, 0.0)
    d = jnp.dot(a_ref[...], w.astype(BF16), preferred_element_type=F32)

    @pl.when(k_step == 0)
    def _():
        o_ref[...] = d

    @pl.when(k_step > 0)
    def _():
        o_ref[...] += d


def _down_kernel(a_ref, w_ref, o_ref, *, tk, k_valid):
    _down_body(a_ref, w_ref, o_ref, pl.program_id(2), tk, k_valid)


def _down(a, w, *, lead, tn, tk, tm_target):
    m, k_pad = a.shape
    k, n = w.shape[-2:]
    assert k_pad % tk == 0 and k_pad - k < tk and n % tn == 0
    tm = _div_tile(m, tm_target, 16)
    nl = len(lead)
    return pl.pallas_call(
        functools.partial(_down_kernel, tk=tk, k_valid=None if k_pad == k else k),
        grid=(m // tm, n // tn, k_pad // tk),
        in_specs=[pl.BlockSpec((tm, tk), lambda i, j, kk: (i, kk)),
                  pl.BlockSpec((None,) * nl + (tk, tn), lambda i, j, kk: lead + (kk, j))],
        out_specs=pl.BlockSpec((tm, tn), lambda i, j, kk: (i, j)),
        out_shape=jax.ShapeDtypeStruct((m, n), F32),
        compiler_params=_params("parallel", "parallel", "arbitrary"),
        name="ffn_down",
    )(a, w)


def _select_by_group(g, values):
    out = values[-1]
    for idx in range(len(values) - 2, -1, -1):
        out = jnp.where(g == idx, values[idx], out)
    return out


def _pool_prompt_kernel(a_ref, halo_ref, w_ref, scale_ref, o_ref, *, tt):
    g, i = pl.program_id(0), pl.program_id(2)
    cur = a_ref[...]
    halo = jnp.where(i > 0, halo_ref[...], 0.0)
    s = jnp.concatenate([halo, cur], axis=0)
    sums, shift = [], 1
    for _ in POOL_WINDOWS:
        s = s + pltpu.roll(s, shift, 0)
        sums.append(s[POOL_HALO:, :])
        shift *= 2
    win = _select_by_group(g, sums)
    wsize = _select_by_group(g, [jnp.int32(w) for w in POOL_WINDOWS])
    pos = i * tt + lax.broadcasted_iota(jnp.int32, (tt, 1), 0)
    inv = 1.0 / jnp.minimum(pos + 1, wsize).astype(F32)
    d = win * inv - cur
    y = jnp.dot(d.astype(BF16), w_ref[...].astype(BF16), preferred_element_type=F32)
    o_ref[...] = (y * scale_ref[...]).astype(o_ref.dtype)


def _pool_prompt(proj, w_pool, pool_scale, *, lead, m_total, n_seq, seq, pw):
    n_g = len(POOL_WINDOWS)
    gw = pw // n_g
    tt = _div_tile(seq, 256, POOL_HALO)
    n_t = seq // tt
    hb = tt // POOL_HALO
    return pl.pallas_call(
        functools.partial(_pool_prompt_kernel, tt=tt),
        grid=(n_g, n_seq, n_t),
        in_specs=[pl.BlockSpec((tt, gw), lambda g, b, i: (b * n_t + i, g)),
                  pl.BlockSpec((POOL_HALO, gw), lambda g, b, i: (jnp.maximum((b * n_t + i) * hb - 1, 0), g)),
                  pl.BlockSpec((None,) * len(lead) + (None, gw, gw), lambda g, b, i: lead + (g, 0, 0)),
                  pl.BlockSpec((1, gw), lambda g, b, i: (lead[0], g))],
        out_specs=pl.BlockSpec((tt, gw), lambda g, b, i: (b * n_t + i, g)),
        out_shape=jax.ShapeDtypeStruct((m_total, pw), BF16),
        compiler_params=_params("parallel", "parallel", "parallel"),
        name="pool_prompt",
    )(proj, proj, w_pool, pool_scale)


def _pool_sample_kernel(a_ref, st_ref, w_ref, scale_ref, ya_hbm_ref, o_ref, *, past):
    del ya_hbm_ref
    g = pl.program_id(0)
    new = a_ref[...]
    n_hist = st_ref.shape[0]
    acc, sums = new, []
    for r in range(n_hist - 1, -1, -1):
        acc = acc + st_ref[r]
        if n_hist - r + 1 in POOL_WINDOWS:
            sums.append(acc)
    win = _select_by_group(g, sums)
    cnt = _select_by_group(g, [jnp.float32(min(past + 1, w)) for w in POOL_WINDOWS])
    d = win * (1.0 / cnt) - new
    y = jnp.dot(d.astype(BF16), w_ref[...].astype(BF16), preferred_element_type=F32)
    o_ref[...] = (y * scale_ref[...]).astype(o_ref.dtype)


def _pool_sample(proj, state_t, w_pool, pool_scale, ya, *, lead, m_prompt, nb, pw, past):
    n_g = len(POOL_WINDOWS)
    gw = pw // n_g
    rb = m_prompt // nb
    n_hist = state_t.shape[0]
    return pl.pallas_call(
        functools.partial(_pool_sample_kernel, past=past),
        grid=(n_g,),
        in_specs=[pl.BlockSpec((nb, gw), lambda g: (rb, g)),
                  pl.BlockSpec((n_hist, nb, gw), lambda g: (0, 0, g)),
                  pl.BlockSpec((None,) * len(lead) + (None, gw, gw), lambda g: lead + (g, 0, 0)),
                  pl.BlockSpec((1, gw), lambda g: (lead[0], g)),
                  pl.BlockSpec(memory_space=pl.ANY)],
        out_specs=pl.BlockSpec((nb, gw), lambda g: (rb, g)),
        out_shape=jax.ShapeDtypeStruct(ya.shape, ya.dtype),
        input_output_aliases={4: 0},
        compiler_params=_params("parallel"),
        name="pool_sample",
    )(proj, state_t, w_pool, pool_scale, ya)


def _sconv_prompt_kernel(b_ref, c_ref, v_ref, ch_ref, vh_ref, w_ref, o_ref, st_ref, *, tt, ksize):
    i = pl.program_id(2)
    z = c_ref[...] * v_ref[...]
    zh = jnp.where(i > 0, ch_ref[...] * vh_ref[...], 0.0)
    ext = jnp.concatenate([zh, z], axis=0)
    conv = w_ref[ksize - 1:ksize, :] * z
    for k in range(ksize - 1):
        shifted = pltpu.roll(ext, ksize - 1 - k, 0)[SCONV_HALO:, :]
        conv = conv + w_ref[k:k + 1, :] * shifted
    o_ref[...] = (b_ref[...] * conv).astype(o_ref.dtype)

    @pl.when(i == pl.num_programs(2) - 1)
    def _():
        st_ref[...] = z[tt - SCONV_HALO:, :]


def _sconv_prompt(proj, sconv_w, *, lead, m_total, n_seq, seq, pw, sw):
    ksize = sconv_w.shape[-2]
    cb = min(512, sw)
    tt = _div_tile(seq, 256, SCONV_HALO)
    n_t = seq // tt
    hb = tt // SCONV_HALO
    ob, oc, ov = pw // cb, (pw + sw) // cb, (pw + 2 * sw) // cb

    def cur(off):
        return pl.BlockSpec((tt, cb), lambda b, j, i: (b * n_t + i, off + j))

    def halo(off):
        return pl.BlockSpec((SCONV_HALO, cb), lambda b, j, i: (jnp.maximum((b * n_t + i) * hb - 1, 0), off + j))

    return pl.pallas_call(
        functools.partial(_sconv_prompt_kernel, tt=tt, ksize=ksize),
        grid=(n_seq, sw // cb, n_t),
        in_specs=[cur(ob), cur(oc), cur(ov), halo(oc), halo(ov),
                  pl.BlockSpec((None,) * len(lead) + (ksize, cb), lambda b, j, i: lead + (0, j))],
        out_specs=[pl.BlockSpec((tt, cb), lambda b, j, i: (b * n_t + i, j)),
                   pl.BlockSpec((None, SCONV_HALO, cb), lambda b, j, i: (b, 0, j))],
        out_shape=[jax.ShapeDtypeStruct((m_total, sw), BF16),
                   jax.ShapeDtypeStruct((n_seq, SCONV_HALO, sw), F32)],
        compiler_params=_params("parallel", "parallel", "arbitrary"),
        name="sconv_prompt",
    )(proj, proj, proj, proj, proj, sconv_w)


def _sconv_sample_kernel(b_ref, c_ref, v_ref, st_ref, w_ref, yb_hbm_ref, o_ref, z_ref, *, ksize):
    del yb_hbm_ref
    z = c_ref[...] * v_ref[...]
    conv = w_ref[ksize - 1:ksize, :] * z
    for k in range(ksize - 1):
        conv = conv + w_ref[k:k + 1, :] * st_ref[k]
    o_ref[...] = (b_ref[...] * conv).astype(o_ref.dtype)
    z_ref[...] = z


def _sconv_sample(proj, state_t, sconv_w, yb, *, lead, m_prompt, nb, pw, sw):
    ksize = sconv_w.shape[-2]
    cb = min(512, sw)
    rb = m_prompt // nb
    ob, oc, ov = pw // cb, (pw + sw) // cb, (pw + 2 * sw) // cb
    return pl.pallas_call(
        functools.partial(_sconv_sample_kernel, ksize=ksize),
        grid=(sw // cb,),
        in_specs=[pl.BlockSpec((nb, cb), lambda j: (rb, ob + j)),
                  pl.BlockSpec((nb, cb), lambda j: (rb, oc + j)),
                  pl.BlockSpec((nb, cb), lambda j: (rb, ov + j)),
                  pl.BlockSpec((ksize - 1, nb, cb), lambda j: (0, 0, j)),
                  pl.BlockSpec((None,) * len(lead) + (ksize, cb), lambda j: lead + (0, j)),
                  pl.BlockSpec(memory_space=pl.ANY)],
        out_specs=[pl.BlockSpec((nb, cb), lambda j: (rb, j)),
                   pl.BlockSpec((nb, cb), lambda j: (0, j))],
        out_shape=[jax.ShapeDtypeStruct(yb.shape, yb.dtype), jax.ShapeDtypeStruct((nb, sw), F32)],
        input_output_aliases={5: 0},
        compiler_params=_params("parallel"),
        name="sconv_sample",
    )(proj, proj, proj, state_t, sconv_w, yb)


def _ln_silu(cv, g, b):
    mu = jnp.mean(cv, axis=-1, keepdims=True)
    var = jnp.mean(jnp.square(cv - mu), axis=-1, keepdims=True)
    return _silu((cv - mu) * lax.rsqrt(var + EPS) * g + b)


def _conf_prompt_kernel(ga_ref, gg_ref, gah_ref, ggh_ref, w_ref, b_ref, lng_ref, lnb_ref, o_ref, st_ref,
                        *, tt, ksize):
    i = pl.program_id(1)
    glu = ga_ref[...] * jax.nn.sigmoid(gg_ref[...])
    gh = jnp.where(i > 0, gah_ref[...] * jax.nn.sigmoid(ggh_ref[...]), 0.0)
    ext = jnp.concatenate([gh, glu], axis=0)
    acc = jnp.zeros_like(glu) + b_ref[...]
    for s in range(SUBLANES):
        es = ext if s == 0 else pltpu.roll(ext, s, 0)
        for q in range(CONF_HALO // SUBLANES):
            shift = SUBLANES * q + s
            if shift <= ksize - 1:
                k = ksize - 1 - shift
                lo = CONF_HALO - SUBLANES * q
                acc = acc + w_ref[k:k + 1, :] * es[lo:lo + tt, :]
    o_ref[...] = _ln_silu(acc, lng_ref[...], lnb_ref[...]).astype(o_ref.dtype)

    @pl.when(i == pl.num_programs(1) - 1)
    def _():
        st_ref[...] = glu[tt - CONF_HALO:, :]


def _conf_prompt(proj, conf_w, conf_b, ln_g, ln_b, *, lead, m_total, n_seq, seq, cw):
    ksize = conf_w.shape[-2]
    assert ksize - 1 <= CONF_HALO - 1
    tt = _div_tile(seq, 128, CONF_HALO)
    n_t = seq // tt
    hb = tt // CONF_HALO
    nl = len(lead)

    def halo(off):
        return pl.BlockSpec((CONF_HALO, cw), lambda b, i: (jnp.maximum((b * n_t + i) * hb - 1, 0), off))

    vec = pl.BlockSpec((None,) * nl + (1, cw), lambda b, i: lead + (0, 0))
    return pl.pallas_call(
        functools.partial(_conf_prompt_kernel, tt=tt, ksize=ksize),
        grid=(n_seq, n_t),
        in_specs=[pl.BlockSpec((tt, cw), lambda b, i: (b * n_t + i, 0)),
                  pl.BlockSpec((tt, cw), lambda b, i: (b * n_t + i, 1)),
                  halo(0), halo(1),
                  pl.BlockSpec((None,) * nl + (ksize, cw), lambda b, i: lead + (0, 0)),
                  vec, vec, vec],
        out_specs=[pl.BlockSpec((tt, cw), lambda b, i: (b * n_t + i, 0)),
                   pl.BlockSpec((None, CONF_HALO, cw), lambda b, i: (b, 0, 0))],
        out_shape=[jax.ShapeDtypeStruct((m_total, cw), BF16),
                   jax.ShapeDtypeStruct((n_seq, CONF_HALO, cw), F32)],
        compiler_params=_params("parallel", "arbitrary"),
        name="conf_prompt",
    )(proj, proj, proj, proj, conf_w,
      conf_b.reshape(conf_b.shape[:nl] + (1, cw)), ln_g.reshape(ln_g.shape[:nl] + (1, cw)),
      ln_b.reshape(ln_b.shape[:nl] + (1, cw)))


def _conf_sample_conv_kernel(ga_ref, gg_ref, st_ref, w_ref, b_ref, cv_ref, glu_ref, *, ksize):
    glu = ga_ref[...] * jax.nn.sigmoid(gg_ref[...])
    acc = w_ref[ksize - 1:ksize, :] * glu + b_ref[...]
    for k in range(ksize - 1):
        acc = acc + w_ref[k:k + 1, :] * st_ref[k]
    cv_ref[...] = acc
    glu_ref[...] = glu


def _conf_sample_norm_kernel(cv_ref, lng_ref, lnb_ref, c_hbm_ref, o_ref):
    del c_hbm_ref
    o_ref[...] = _ln_silu(cv_ref[...], lng_ref[...], lnb_ref[...]).astype(o_ref.dtype)


def _conf_sample(proj, state_t, conf_w, conf_b, ln_g, ln_b, c_out, *, lead, m_prompt, nb, cw):
    ksize = conf_w.shape[-2]
    cb = min(512, cw)
    rb = m_prompt // nb
    nl = len(lead)
    cv, glu = pl.pallas_call(
        functools.partial(_conf_sample_conv_kernel, ksize=ksize),
        grid=(cw // cb,),
        in_specs=[pl.BlockSpec((nb, cb), lambda j: (rb, j)),
                  pl.BlockSpec((nb, cb), lambda j: (rb, cw // cb + j)),
                  pl.BlockSpec((ksize - 1, nb, cb), lambda j: (0, 0, j)),
                  pl.BlockSpec((None,) * nl + (ksize, cb), lambda j: lead + (0, j)),
                  pl.BlockSpec((None,) * nl + (1, cb), lambda j: lead + (0, j))],
        out_specs=[pl.BlockSpec((nb, cb), lambda j: (0, j)), pl.BlockSpec((nb, cb), lambda j: (0, j))],
        out_shape=[jax.ShapeDtypeStruct((nb, cw), F32), jax.ShapeDtypeStruct((nb, cw), F32)],
        compiler_params=_params("parallel"),
        name="conf_sample_conv",
    )(proj, proj, state_t, conf_w, conf_b.reshape(conf_b.shape[:nl] + (1, cw)))
    vec = pl.BlockSpec((None,) * nl + (1, cw), lambda j: lead + (0, 0))
    c_out = pl.pallas_call(
        _conf_sample_norm_kernel,
        grid=(1,),
        in_specs=[pl.BlockSpec((nb, cw), lambda j: (0, 0)), vec, vec, pl.BlockSpec(memory_space=pl.ANY)],
        out_specs=pl.BlockSpec((nb, cw), lambda j: (rb, 0)),
        out_shape=jax.ShapeDtypeStruct(c_out.shape, c_out.dtype),
        input_output_aliases={3: 0},
        compiler_params=_params("arbitrary"),
        name="conf_sample_norm",
    )(cv, ln_g.reshape(ln_g.shape[:nl] + (1, cw)), ln_b.reshape(ln_b.shape[:nl] + (1, cw)), c_out)
    return c_out, glu


def _rope_tables(pos, rope_dim):
    half = rope_dim // 2
    inv = ROPE_THETA ** (-jnp.arange(half, dtype=F32) * (2.0 / rope_dim))
    ang = pos.astype(F32)[:, None] * inv[None, :]
    cos, sin = jnp.cos(ang), jnp.sin(ang)
    zeros = jnp.zeros((pos.shape[0], LANES - rope_dim), F32)
    zh = jnp.zeros_like(sin)
    t_cos = jnp.concatenate([cos, cos, zeros], axis=1)
    t_up = jnp.concatenate([zh, sin, zeros], axis=1)
    t_down = jnp.concatenate([-sin, zh, zeros], axis=1)
    return t_cos, t_up, t_down


def _rope_lanes(v, t_cos, t_up, t_down, half):
    return v * t_cos + pltpu.roll(v, half, 1) * t_up + pltpu.roll(v, LANES - half, 1) * t_down


def _qproj_kernel(a_ref, w_ref, tc_ref, tu_ref, td_ref, o_ref, *, heads_per_tile, half):
    d = jnp.dot(a_ref[...], w_ref[...].astype(BF16), preferred_element_type=F32)
    tc, tu, td = tc_ref[...], tu_ref[...], td_ref[...]
    pieces = []
    for h in range(heads_per_tile):
        base = 2 * LANES * h
        pieces.append(d[:, base:base + LANES])
        pieces.append(_rope_lanes(d[:, base + LANES:base + 2 * LANES], tc, tu, td, half))
    o_ref[...] = jnp.concatenate(pieces, axis=1).astype(o_ref.dtype)


def _qproj(qn, w_uq_pad, tables, *, half):
    m, k = qn.shape
    n = w_uq_pad.shape[1]
    tn = min(512, n)
    tm = _div_tile(m, 1040, 16)
    tab = pl.BlockSpec((tm, LANES), lambda i, j: (i, 0))
    return pl.pallas_call(
        functools.partial(_qproj_kernel, heads_per_tile=tn // (2 * LANES), half=half),
        grid=(m // tm, n // tn),
        in_specs=[pl.BlockSpec((tm, k), lambda i, j: (i, 0)),
                  pl.BlockSpec((k, tn), lambda i, j: (0, j)), tab, tab, tab],
        out_specs=pl.BlockSpec((tm, tn), lambda i, j: (i, j)),
        out_shape=jax.ShapeDtypeStruct((m, n), BF16),
        compiler_params=_params("parallel", "arbitrary"),
        name="q_proj_rope",
    )(qn, w_uq_pad, *tables)


def _kv_kernel(kva_ref, kpe_ref, g_ref, tc_ref, tu_ref, td_ref, ckv_ref, ckvb_ref, kpe_o_ref, kpeb_ref, *, half):
    ckv = _rms(kva_ref[...], g_ref[...])
    ckv_ref[...] = ckv
    ckvb_ref[...] = ckv.astype(BF16)
    kpe = _rope_lanes(kpe_ref[...], tc_ref[...], tu_ref[...], td_ref[...], half)
    kpe_o_ref[...] = kpe
    kpeb_ref[...] = kpe.astype(BF16)


def _kv_finish(proj, kpe_raw, kv_norm_g, tables, *, kvl, col_block, half):
    m = proj.shape[0]
    tr = _div_tile(m, 256, 16)
    tab = pl.BlockSpec((tr, LANES), lambda i: (i, 0))
    lat = pl.BlockSpec((tr, kvl), lambda i: (i, 0))
    return pl.pallas_call(
        functools.partial(_kv_kernel, half=half),
        grid=(m // tr,),
        in_specs=[pl.BlockSpec((tr, kvl), lambda i: (i, col_block)), tab,
                  pl.BlockSpec((1, kvl), lambda i: (0, 0)), tab, tab, tab],
        out_specs=[lat, lat, tab, tab],
        out_shape=[jax.ShapeDtypeStruct((m, kvl), F32), jax.ShapeDtypeStruct((m, kvl), BF16),
                   jax.ShapeDtypeStruct((m, LANES), F32), jax.ShapeDtypeStruct((m, LANES), BF16)],
        compiler_params=_params("parallel"),
        name="kv_finish",
    )(proj, kpe_raw, kv_norm_g.reshape(1, kvl), *tables)


def _kproj_kernel(a_ref, w_ref, kpe_ref, o_ref, *, heads_per_tile):
    d = jnp.dot(a_ref[...], w_ref[...].astype(BF16), preferred_element_type=F32).astype(o_ref.dtype)
    kpe = kpe_ref[...]
    pieces = []
    for h in range(heads_per_tile):
        pieces.append(d[:, 2 * LANES * h:2 * LANES * h + LANES])
        pieces.append(kpe)
    o_ref[...] = jnp.concatenate(pieces, axis=1)


def _kproj(ckv_b, w_uk_pad, kpe_b, *, m_rows):
    k = ckv_b.shape[1]
    n = w_uk_pad.shape[1]
    tn = min(512, n)
    tm = _div_tile(m_rows, 1024, 16)
    return pl.pallas_call(
        functools.partial(_kproj_kernel, heads_per_tile=tn // (2 * LANES)),
        grid=(m_rows // tm, n // tn),
        in_specs=[pl.BlockSpec((tm, k), lambda i, j: (i, 0)),
                  pl.BlockSpec((k, tn), lambda i, j: (0, j)),
                  pl.BlockSpec((tm, LANES), lambda i, j: (i, 0))],
        out_specs=pl.BlockSpec((tm, tn), lambda i, j: (i, j)),
        out_shape=jax.ShapeDtypeStruct((m_rows, n), BF16),
        compiler_params=_params("parallel", "arbitrary"),
        name="k_proj",
    )(ckv_b, w_uk_pad, kpe_b)


def _flash_kernel(q_ref, k_ref, v_ref, o_ref, m_ref, l_ref, acc_ref, *, hg, tq, tk, vd, scale):
    qi, ki = pl.program_id(2), pl.program_id(3)

    @pl.when(ki == 0)
    def _():
        m_ref[...] = jnp.full(m_ref.shape, -jnp.inf, F32)
        l_ref[...] = jnp.zeros(l_ref.shape, F32)
        acc_ref[...] = jnp.zeros(acc_ref.shape, F32)

    @pl.when(ki * tk <= qi * tq + tq - 1)
    def _():
        qpos = qi * tq + lax.broadcasted_iota(jnp.int32, (tq, tk), 0)
        kpos = ki * tk + lax.broadcasted_iota(jnp.int32, (tq, tk), 1)
        mask = kpos <= qpos
        for h in range(hg):
            q = q_ref[:, 2 * LANES * h:2 * LANES * (h + 1)]
            k = k_ref[:, 2 * LANES * h:2 * LANES * (h + 1)]
            s = lax.dot_general(q, k, (((1,), (1,)), ((), ())), preferred_element_type=F32)
            s = jnp.where(mask, s * scale, NEG_INF)
            m_prev = m_ref[h]
            m_new = jnp.maximum(m_prev, jnp.max(s, axis=-1, keepdims=True))
            alpha = jnp.exp(m_prev - m_new)
            p = jnp.exp(s - m_new)
            l_ref[h] = alpha * l_ref[h] + jnp.sum(p, axis=-1, keepdims=True)
            pv = jnp.dot(p.astype(BF16), v_ref[:, vd * h:vd * (h + 1)], preferred_element_type=F32)
            acc_ref[h] = alpha * acc_ref[h] + pv
            m_ref[h] = m_new

    @pl.when(ki == pl.num_programs(3) - 1)
    def _():
        for h in range(hg):
            o_ref[:, vd * h:vd * (h + 1)] = (acc_ref[h] / l_ref[h]).astype(o_ref.dtype)


def _prompt_attention(q_cat, k_cat, v, *, m_total, n_seq, seq, n_heads, vd, scale):
    hg = math.gcd(n_heads, 4)
    tq = tk = _div_tile(seq, 512, 128)
    n_q, n_k = seq // tq, seq // tk

    def last_needed(qi):
        return (qi * tq + tq - 1) // tk

    return pl.pallas_call(
        functools.partial(_flash_kernel, hg=hg, tq=tq, tk=tk, vd=vd, scale=scale),
        grid=(n_seq, n_heads // hg, n_q, n_k),
        in_specs=[pl.BlockSpec((tq, hg * 2 * LANES), lambda b, g, qi, ki: (b * n_q + qi, g)),
                  pl.BlockSpec((tk, hg * 2 * LANES),
                               lambda b, g, qi, ki: (b * n_k + jnp.minimum(ki, last_needed(qi)), g)),
                  pl.BlockSpec((tk, hg * vd),
                               lambda b, g, qi, ki: (b * n_k + jnp.minimum(ki, last_needed(qi)), g))],
        out_specs=pl.BlockSpec((tq, hg * vd), lambda b, g, qi, ki: (b * n_q + qi, g)),
        out_shape=jax.ShapeDtypeStruct((m_total, n_heads * vd), BF16),
        scratch_shapes=[pltpu.VMEM((hg, tq, 1), F32), pltpu.VMEM((hg, tq, 1), F32),
                        pltpu.VMEM((hg, tq, vd), F32)],
        compiler_params=_params("parallel", "parallel", "parallel", "arbitrary"),
        name="prompt_attention",
    )(q_cat, k_cat, v)


def _qabsorb_kernel(q_ref, w_ref, o_ref):
    o_ref[...] = lax.dot_general(q_ref[...], w_ref[...].astype(BF16), (((1,), (1,)), ((), ())),
                                 preferred_element_type=F32).astype(o_ref.dtype)


def _qabsorb(q_cat, w_uk2, *, m_prompt, nb, n_heads, nope):
    kvl = w_uk2.shape[0]
    rb = m_prompt // nb
    return pl.pallas_call(
        _qabsorb_kernel,
        grid=(n_heads,),
        in_specs=[pl.BlockSpec((nb, nope), lambda h: (rb, 2 * h)),
                  pl.BlockSpec((kvl, nope), lambda h: (0, h))],
        out_specs=pl.BlockSpec((nb, kvl), lambda h: (0, h)),
        out_shape=jax.ShapeDtypeStruct((nb, n_heads * kvl), BF16),
        compiler_params=_params("parallel"),
        name="q_absorb",
    )(q_cat, w_uk2)


def _paged_kernel(pt_ref, qlat_ref, qpe_ref, cn_ref, pn_ref, *rest, n_pp, ps, rope_dim, scale):
    del pt_ref
    ckv_refs, kpe_refs = rest[:n_pp], rest[n_pp:2 * n_pp]
    o_ref, keys_ref, s_ref = rest[2 * n_pp:]
    c = pl.program_id(1)
    n_chunks = s_ref.shape[0]
    qlat = qlat_ref[...]
    qpe = qpe_ref[...]

    scores = []
    for r in range(n_pp):
        kc = ckv_refs[r][...].astype(BF16)
        kp_t = kpe_refs[r][...].astype(BF16)
        keys_ref[c * n_pp + r] = kc
        s = lax.dot_general(qlat, kc, (((1,), (1,)), ((), ())), preferred_element_type=F32)
        s = s + jnp.dot(qpe[:, :rope_dim], kp_t, preferred_element_type=F32)
        scores.append(s)
    s_ref[c] = jnp.concatenate(scores, axis=1) * scale

    @pl.when(c == n_chunks - 1)
    def _():
        cn = cn_ref[...].astype(F32)
        pn = pn_ref[...].astype(F32)
        s_new = (jnp.sum(qlat.astype(F32) * cn, axis=-1, keepdims=True)
                 + jnp.sum(qpe.astype(F32) * pn, axis=-1, keepdims=True)) * scale
        m = s_new
        for i in range(n_chunks):
            m = jnp.maximum(m, jnp.max(s_ref[i], axis=-1, keepdims=True))
        e_new = jnp.exp(s_new - m)
        total = e_new
        for i in range(n_chunks):
            total = total + jnp.sum(jnp.exp(s_ref[i] - m), axis=-1, keepdims=True)
        acc = (e_new / total).astype(BF16).astype(F32) * cn
        for i in range(n_chunks):
            p = (jnp.exp(s_ref[i] - m) / total).astype(BF16)
            k_chunk = keys_ref[i * n_pp:(i + 1) * n_pp].reshape(n_pp * ps, keys_ref.shape[-1])
            acc = acc + jnp.dot(p, k_chunk, preferred_element_type=F32)
        o_ref[...] = acc.astype(o_ref.dtype)


def _sample_attention(q_lat, q_pe, ckv_new, kpe_new, cache_ckv, cache_kpe_t, page_table, *, layer, scale):
    nb, n_heads, kvl = q_lat.shape
    n_pages = page_table.shape[1]
    ps = cache_ckv.shape[2]
    rope_dim = cache_kpe_t.shape[2]
    n_pp = math.gcd(n_pages, 8)

    def page_spec(rows, width, r):
        return pl.BlockSpec((None, None, rows, width),
                            lambda n, c, pt: (layer, pt[n * n_pages + c * n_pp + r], 0, 0))

    in_specs = [pl.BlockSpec((None, n_heads, kvl), lambda n, c, pt: (n, 0, 0)),
                pl.BlockSpec((None, n_heads, LANES), lambda n, c, pt: (n, 0, 0)),
                pl.BlockSpec((None, 1, kvl), lambda n, c, pt: (n, 0, 0)),
                pl.BlockSpec((None, 1, LANES), lambda n, c, pt: (n, 0, 0))]
    in_specs += [page_spec(ps, kvl, r) for r in range(n_pp)]
    in_specs += [page_spec(rope_dim, ps, r) for r in range(n_pp)]
    grid_spec = pltpu.PrefetchScalarGridSpec(
        num_scalar_prefetch=1,
        grid=(nb, n_pages // n_pp),
        in_specs=in_specs,
        out_specs=pl.BlockSpec((None, n_heads, kvl), lambda n, c, pt: (n, 0, 0)),
        scratch_shapes=[pltpu.VMEM((n_pages, ps, kvl), BF16),
                        pltpu.VMEM((n_pages // n_pp, n_heads, n_pp * ps), F32)],
    )
    return pl.pallas_call(
        functools.partial(_paged_kernel, n_pp=n_pp, ps=ps, rope_dim=rope_dim, scale=scale),
        grid_spec=grid_spec,
        out_shape=jax.ShapeDtypeStruct((nb, n_heads, kvl), BF16),
        compiler_params=_params("parallel", "arbitrary"),
        name="sample_attention",
    )(page_table.reshape(-1), q_lat, q_pe, ckv_new, kpe_new,
      *([cache_ckv] * n_pp), *([cache_kpe_t] * n_pp))


def _vup_kernel(o_ref_in, w_ref, attn_hbm_ref, o_ref):
    del attn_hbm_ref
    o_ref[...] = jnp.dot(o_ref_in[...], w_ref[...].astype(BF16), preferred_element_type=F32).astype(o_ref.dtype)


def _sample_vup(o_lat2, w_uv2, attn, *, m_prompt, nb, n_heads, vd):
    kvl = w_uv2.shape[0]
    rb = m_prompt // nb
    return pl.pallas_call(
        _vup_kernel,
        grid=(n_heads,),
        in_specs=[pl.BlockSpec((nb, kvl), lambda h: (0, h)),
                  pl.BlockSpec((kvl, vd), lambda h: (0, h)),
                  pl.BlockSpec(memory_space=pl.ANY)],
        out_specs=pl.BlockSpec((nb, vd), lambda h: (rb, h)),
        out_shape=jax.ShapeDtypeStruct(attn.shape, attn.dtype),
        input_output_aliases={2: 0},
        compiler_params=_params("parallel"),
        name="sample_v_up",
    )(o_lat2, w_uv2, attn)


def _router_kernel(h_ref, w_ref, b_ref, idx_ref, gate_ref, *, n_experts):
    logits = jnp.dot(h_ref[...].astype(BF16), w_ref[...].astype(BF16), preferred_element_type=F32) + b_ref[...]
    lane = lax.broadcasted_iota(jnp.int32, logits.shape, 1)
    logits = jnp.where(lane < n_experts, logits, -jnp.inf)
    m1 = jnp.max(logits, axis=-1, keepdims=True)
    i1 = jnp.min(jnp.where(logits == m1, lane, LANES), axis=-1, keepdims=True)
    rest = jnp.where(lane == i1, -jnp.inf, logits)
    m2 = jnp.max(rest, axis=-1, keepdims=True)
    i2 = jnp.min(jnp.where(rest == m2, lane, LANES), axis=-1, keepdims=True)
    e2 = jnp.exp(m2 - m1)
    denom = 1.0 + e2
    idx_ref[...] = jnp.where(lane == 0, i1, jnp.where(lane == 1, i2, 0))
    gate_ref[...] = jnp.where(lane == 0, 1.0 / denom, jnp.where(lane == 1, e2 / denom, 0.0))


def _router(h, router_w_pad, router_b_pad, *, n_experts):
    m, d = h.shape
    tr = _div_tile(m, 256, 8)
    row = pl.BlockSpec((tr, LANES), lambda i: (i, 0))
    return pl.pallas_call(
        functools.partial(_router_kernel, n_experts=n_experts),
        grid=(m // tr,),
        in_specs=[pl.BlockSpec((tr, d), lambda i: (i, 0)),
                  pl.BlockSpec((d, LANES), lambda i: (0, 0)),
                  pl.BlockSpec((1, LANES), lambda i: (0, 0))],
        out_specs=[row, row],
        out_shape=[jax.ShapeDtypeStruct((m, LANES), jnp.int32), jax.ShapeDtypeStruct((m, LANES), F32)],
        compiler_params=_params("parallel"),
        name="router_top2",
    )(h, router_w_pad, router_b_pad)


def _row_copy(src_hbm, row, dst, r, sem):
    return pltpu.make_async_copy(src_hbm.at[pl.ds(row, 1), :], dst.at[pl.ds(r, 1), :], sem)


def _gather_kernel(src_ref, nu_ref, h_hbm, o_ref, buf, sem, *, rows):
    i = pl.program_id(0)

    @pl.when(i < nu_ref[0])
    def _():
        base = i * rows

        def issue(r, carry):
            _row_copy(h_hbm, src_ref[base + r], buf, r, sem).start()
            return carry

        def drain(r, carry):
            _row_copy(h_hbm, 0, buf, r, sem).wait()
            return carry

        lax.fori_loop(0, rows, issue, 0)
        lax.fori_loop(0, rows, drain, 0)
        o_ref[...] = buf[...].astype(o_ref.dtype)


def _gather_rows(h, src_rows, n_chunks_used, *, rows):
    mp = src_rows.shape[0]
    d = h.shape[1]
    grid_spec = pltpu.PrefetchScalarGridSpec(
        num_scalar_prefetch=2,
        grid=(mp // rows,),
        in_specs=[pl.BlockSpec(memory_space=pl.ANY)],
        out_specs=pl.BlockSpec((rows, d), lambda i, src, nu: (jnp.minimum(i, nu[0] - 1), 0)),
        scratch_shapes=[pltpu.VMEM((rows, d), F32), pltpu.SemaphoreType.DMA(())],
    )
    return pl.pallas_call(
        functools.partial(_gather_kernel, rows=rows),
        grid_spec=grid_spec,
        out_shape=jax.ShapeDtypeStruct((mp, d), BF16),
        compiler_params=_params("arbitrary"),
        name="moe_gather",
    )(src_rows, n_chunks_used, h)


def _moe_gateup_kernel(te_ref, nu_ref, rows_ref, a_ref, w1_ref, w3_ref, o_ref, *, chunk):
    del te_ref
    i = pl.program_id(0)

    @pl.when(i < nu_ref[0])
    def _():
        n_valid = rows_ref[i]
        w1 = w1_ref[...].astype(BF16)
        w3 = w3_ref[...].astype(BF16)
        for c in range(o_ref.shape[0] // chunk):
            rows = slice(c * chunk, (c + 1) * chunk)

            @pl.when(c * chunk < n_valid)
            def _(rows=rows):
                a = a_ref[rows, :]
                g = jnp.dot(a, w1, preferred_element_type=F32)
                u = jnp.dot(a, w3, preferred_element_type=F32)
                o_ref[rows, :] = (_silu(g) * u).astype(o_ref.dtype)

            @pl.when(c * chunk >= n_valid)
            def _(rows=rows):
                o_ref[rows, :] = jnp.zeros((chunk, o_ref.shape[1]), o_ref.dtype)


def _moe_gateup(xs, w1, w3, tile_expert, n_used, tile_rows, *, layer, tm, tn, chunk):
    mp, d = xs.shape
    f = w1.shape[-1]
    assert f % tn == 0 and tm % chunk == 0
    n_j = f // tn

    def ii(i, nu):
        return jnp.minimum(i, nu[0] - 1)

    def jj(i, j, nu):
        return jnp.where(i < nu[0], j, n_j - 1)

    wspec = pl.BlockSpec((None, None, d, tn), lambda i, j, te, nu, tr: (layer, te[ii(i, nu)], 0, jj(i, j, nu)))
    grid_spec = pltpu.PrefetchScalarGridSpec(
        num_scalar_prefetch=3,
        grid=(mp // tm, n_j),
        in_specs=[pl.BlockSpec((tm, d), lambda i, j, te, nu, tr: (ii(i, nu), 0)), wspec, wspec],
        out_specs=pl.BlockSpec((tm, tn), lambda i, j, te, nu, tr: (ii(i, nu), jj(i, j, nu))),
    )
    return pl.pallas_call(
        functools.partial(_moe_gateup_kernel, chunk=chunk),
        grid_spec=grid_spec,
        out_shape=jax.ShapeDtypeStruct((mp, f), BF16),
        compiler_params=_params("arbitrary", "arbitrary"),
        name="moe_gateup",
    )(tile_expert, n_used, tile_rows, xs, w1, w3)


def _moe_down_kernel(te_ref, nu_ref, rows_ref, a_ref, w_ref, o_ref, *, chunk):
    del te_ref
    i, k = pl.program_id(0), pl.program_id(2)

    @pl.when(i < nu_ref[0])
    def _():
        n_valid = rows_ref[i]
        w = w_ref[...].astype(BF16)
        for c in range(o_ref.shape[0] // chunk):
            rows = slice(c * chunk, (c + 1) * chunk)

            @pl.when(jnp.logical_and(c * chunk < n_valid, k == 0))
            def _(rows=rows):
                o_ref[rows, :] = jnp.dot(a_ref[rows, :], w, preferred_element_type=F32)

            @pl.when(jnp.logical_and(c * chunk < n_valid, k > 0))
            def _(rows=rows):
                o_ref[rows, :] += jnp.dot(a_ref[rows, :], w, preferred_element_type=F32)

            @pl.when(jnp.logical_and(c * chunk >= n_valid, k == 0))
            def _(rows=rows):
                o_ref[rows, :] = jnp.zeros((chunk, o_ref.shape[1]), o_ref.dtype)


def _moe_down(act, w2, tile_expert, n_used, tile_rows, *, layer, tm, tn, tk, chunk):
    mp, f = act.shape
    d = w2.shape[-1]
    assert f % tk == 0 and d % tn == 0
    n_j, n_k = d // tn, f // tk

    def ii(i, nu):
        return jnp.minimum(i, nu[0] - 1)

    def jj(i, j, nu):
        return jnp.where(i < nu[0], j, n_j - 1)

    def kk(i, k, nu):
        return jnp.where(i < nu[0], k, n_k - 1)

    assert tm % chunk == 0
    grid_spec = pltpu.PrefetchScalarGridSpec(
        num_scalar_prefetch=3,
        grid=(mp // tm, n_j, n_k),
        in_specs=[pl.BlockSpec((tm, tk), lambda i, j, k, te, nu, tr: (ii(i, nu), kk(i, k, nu))),
                  pl.BlockSpec((None, None, tk, tn),
                               lambda i, j, k, te, nu, tr: (layer, te[ii(i, nu)], kk(i, k, nu), jj(i, j, nu)))],
        out_specs=pl.BlockSpec((tm, tn), lambda i, j, k, te, nu, tr: (ii(i, nu), jj(i, j, nu))),
    )
    return pl.pallas_call(
        functools.partial(_moe_down_kernel, chunk=chunk),
        grid_spec=grid_spec,
        out_shape=jax.ShapeDtypeStruct((mp, d), F32),
        compiler_params=_params("arbitrary", "arbitrary", "arbitrary"),
        name="moe_down",
    )(tile_expert, n_used, tile_rows, act, w2)


def _combine_kernel(dest_ref, o_hbm, gate_ref, x_ref, g_ref, xo_ref, buf0, buf1, sem, *, rows):
    base = pl.program_id(0) * rows

    def issue(r, carry):
        _row_copy(o_hbm, dest_ref[2 * (base + r)], buf0, r, sem).start()
        _row_copy(o_hbm, dest_ref[2 * (base + r) + 1], buf1, r, sem).start()
        return carry

    def drain(r, carry):
        _row_copy(o_hbm, 0, buf0, r, sem).wait()
        _row_copy(o_hbm, 0, buf1, r, sem).wait()
        return carry

    lax.fori_loop(0, rows, issue, 0)
    lax.fori_loop(0, rows, drain, 0)
    gates = gate_ref[...]
    y = gates[:, 0:1] * buf0[...] + gates[:, 1:2] * buf1[...]
    xo_ref[...] = x_ref[...] + _rms(y, g_ref[...])


def _moe_combine(o_sorted, dest, gates, x, g_post):
    m, d = x.shape
    rows = _div_tile(m, 128, 8)
    grid_spec = pltpu.PrefetchScalarGridSpec(
        num_scalar_prefetch=1,
        grid=(m // rows,),
        in_specs=[pl.BlockSpec(memory_space=pl.ANY),
                  pl.BlockSpec((rows, LANES), lambda i, dest: (i, 0)),
                  pl.BlockSpec((rows, d), lambda i, dest: (i, 0)),
                  pl.BlockSpec((1, d), lambda i, dest: (0, 0))],
        out_specs=pl.BlockSpec((rows, d), lambda i, dest: (i, 0)),
        scratch_shapes=[pltpu.VMEM((rows, d), F32), pltpu.VMEM((rows, d), F32), pltpu.SemaphoreType.DMA(())],
    )
    return pl.pallas_call(
        functools.partial(_combine_kernel, rows=rows),
        grid_spec=grid_spec,
        out_shape=jax.ShapeDtypeStruct((m, d), F32),
        compiler_params=_params("arbitrary"),
        name="moe_combine",
    )(dest, o_sorted, gates, x, g_post.reshape(1, d))


def _route_plan(idx, *, n_experts, tm, n_tiles_max):
    m = idx.shape[0]
    e_flat = idx[:, :2].reshape(-1)
    onehot = (e_flat[:, None] == jnp.arange(n_experts, dtype=jnp.int32)[None, :]).astype(jnp.int32)
    csum = jnp.cumsum(onehot, axis=0)
    rank = jnp.take_along_axis(csum, e_flat[:, None], axis=1)[:, 0] - 1
    counts = csum[-1]
    tiles = (counts + tm - 1) // tm
    tile_end = jnp.cumsum(tiles)
    row_start = (tile_end - tiles) * tm
    dest = (row_start[e_flat] + rank).astype(jnp.int32)
    n_used = tile_end[-1:].astype(jnp.int32)
    tile_expert = jnp.minimum(
        jnp.searchsorted(tile_end, jnp.arange(n_tiles_max, dtype=jnp.int32), side="right"),
        n_experts - 1).astype(jnp.int32)
    src_rows = jnp.zeros((n_tiles_max * tm,), jnp.int32).at[dest].set(jnp.arange(2 * m, dtype=jnp.int32) // 2)
    tile_local = jnp.arange(n_tiles_max, dtype=jnp.int32) - (tile_end - tiles)[tile_expert]
    tile_rows = jnp.clip(counts[tile_expert] - tile_local * tm, 0, tm).astype(jnp.int32)
    return dest, src_rows, tile_expert, n_used, tile_rows


def kernel(x_prompt, x_sample, state_pool, state_sconv, state_conf, cache_ckv, cache_kpe, page_table, norm_mix_pre, norm_mix_post, norm_ffn_pre, norm_ffn_post, w_in0, w_pool, pool_scale, sconv_w, w_out0, w_in1, conf_w, conf_b, conf_ln_g, conf_ln_b, q_norm_g, w_uq, kv_norm_g, w_uk, w_uv, w_out1, ffn_w1, ffn_w3, ffn_w2, router_w, router_b, moe_w1, moe_w3, moe_w2):
    n_seq, seq, d = x_prompt.shape
    nb, t_s, _ = x_sample.shape
    assert t_s == 1 and norm_mix_pre.shape[0] == 2
    m_p = n_seq * seq
    m = m_p + nb
    assert m_p % nb == 0 and nb % 16 == 0
    pw = w_pool.shape[1] * w_pool.shape[2]
    sw = sconv_w.shape[-1]
    cw = conf_w.shape[-1]
    ql = q_norm_g.shape[-1]
    kvl = kv_norm_g.shape[-1]
    n_heads = w_uq.shape[2]
    rope_dim = cache_kpe.shape[-1]
    nope = w_uq.shape[3] - rope_dim
    vd = w_uv.shape[-1]
    n_experts = router_w.shape[-1]
    assert w_pool.shape[1] == len(POOL_WINDOWS) and state_pool.shape[2] == max(POOL_WINDOWS) - 1
    assert nope == LANES and vd == LANES and rope_dim <= LANES // 2
    n_pages, ps = page_table.shape[1], cache_ckv.shape[2]
    past = n_pages * ps
    scale = float(nope + rope_dim) ** -0.5
    half = rope_dim // 2

    x = jnp.concatenate([x_prompt.reshape(m_p, d), x_sample.reshape(nb, d)], axis=0)

    h = _rms_cast(x, norm_mix_pre[0])
    proj = _matmul([h], w_in0, lead=(0,), name="in_proj0")
    ya = _pool_prompt(proj, w_pool, pool_scale, lead=(0,), m_total=m, n_seq=n_seq, seq=seq, pw=pw)
    ya = _pool_sample(proj, jnp.swapaxes(state_pool[0], 0, 1), w_pool, pool_scale, ya,
                      lead=(0,), m_prompt=m_p, nb=nb, pw=pw, past=past)
    yb, z_tail = _sconv_prompt(proj, sconv_w, lead=(0,), m_total=m, n_seq=n_seq, seq=seq, pw=pw, sw=sw)
    yb, z_new = _sconv_sample(proj, jnp.swapaxes(state_sconv[0], 0, 1), sconv_w, yb,
                              lead=(0,), m_prompt=m_p, nb=nb, pw=pw, sw=sw)
    y = _matmul([ya, yb], w_out0, lead=(0,), name="out_proj0")
    x, h = _resid_norm(x, y, norm_mix_post[0], norm_ffn_pre[0], BF16)
    act = _gateup(h, ffn_w1, ffn_w3, lead=(0,), tn=256, pad_to=1024, tm_target=1040)
    f = _down(act, ffn_w2, lead=(0,), tn=1024, tk=1024, tm_target=2080)
    x, h = _resid_norm(x, f, norm_ffn_post[0], norm_mix_pre[1], BF16)

    n_hist_pool = state_pool.shape[2]
    k_sconv = sconv_w.shape[-2]
    a_p = proj[:m_p, :pw].reshape(n_seq, seq, pw)
    new_pool_p = a_p[:, seq - n_hist_pool:, :][None]
    new_pool_s = jnp.concatenate([state_pool[0][:, 1:, :], proj[m_p:, None, :pw]], axis=1)[None]
    new_sconv_p = z_tail[:, SCONV_HALO - (k_sconv - 1):, :][None]
    new_sconv_s = jnp.concatenate([state_sconv[0][:, 1:, :], z_new[:, None, :]], axis=1)[None]

    main_cols = 2 * cw + ql + kvl
    proj = _matmul([h], w_in1, lead=(0,), n_cols=main_cols, tn=_div_tile(main_cols, 512, LANES), name="in_proj1")
    w_kpe = jnp.pad(w_in1[0][:, main_cols:], ((0, 0), (0, LANES - rope_dim)))
    kpe_raw = _matmul([h], w_kpe, name="in_proj1_rope_key")

    c_out, glu_tail = _conf_prompt(proj, conf_w, conf_b, conf_ln_g, conf_ln_b,
                                   lead=(0,), m_total=m, n_seq=n_seq, seq=seq, cw=cw)
    c_out, glu_new = _conf_sample(proj, jnp.swapaxes(state_conf[0], 0, 1), conf_w, conf_b, conf_ln_g, conf_ln_b,
                                  c_out, lead=(0,), m_prompt=m_p, nb=nb, cw=cw)

    pos = jnp.concatenate([jnp.tile(jnp.arange(seq, dtype=jnp.int32), n_seq),
                           jnp.full((nb,), past, jnp.int32)])
    tables = _rope_tables(pos, rope_dim)
    qn = _rms_cast(proj, q_norm_g[0], width=ql, col_block=2 * cw // ql)
    w_uq_pad = jnp.pad(w_uq[0], ((0, 0), (0, 0), (0, 2 * LANES - nope - rope_dim))).reshape(ql, n_heads * 2 * LANES)
    q_cat = _qproj(qn, w_uq_pad, tables, half=half)
    ckv, ckv_b, kpe, kpe_b = _kv_finish(proj, kpe_raw, kv_norm_g[0], tables,
                                        kvl=kvl, col_block=(2 * cw + ql) // kvl, half=half)

    w_uk_pad = jnp.pad(w_uk[0], ((0, 0), (0, 0), (0, LANES))).reshape(kvl, n_heads * 2 * LANES)
    k_cat = _kproj(ckv_b, w_uk_pad, kpe_b, m_rows=m_p)
    w_uv2 = w_uv[0].reshape(kvl, n_heads * vd)
    v_p = _matmul([ckv_b[:m_p]], w_uv2, out_dtype=BF16, tm_target=1024, name="v_proj")
    attn = _prompt_attention(q_cat, k_cat, v_p, m_total=m, n_seq=n_seq, seq=seq, n_heads=n_heads, vd=vd, scale=scale)

    w_uk2 = w_uk[0].reshape(kvl, n_heads * nope)
    q_lat = _qabsorb(q_cat, w_uk2, m_prompt=m_p, nb=nb, n_heads=n_heads, nope=nope)
    q_pe_s = q_cat[m_p:].reshape(nb, n_heads, 2 * LANES)[:, :, LANES:]
    o_lat = _sample_attention(q_lat.reshape(nb, n_heads, kvl), q_pe_s, ckv_b[m_p:].reshape(nb, 1, kvl),
                              kpe_b[m_p:].reshape(nb, 1, LANES), cache_ckv, jnp.swapaxes(cache_kpe, 2, 3),
                              page_table, layer=0, scale=scale)
    attn = _sample_vup(o_lat.reshape(nb, n_heads * kvl), w_uv2, attn, m_prompt=m_p, nb=nb, n_heads=n_heads, vd=vd)

    y = _matmul([c_out, attn], w_out1, lead=(0,), tn=256, name="out_proj1")
    x, h32 = _resid_norm(x, y, norm_mix_post[1], norm_ffn_pre[1], F32)

    router_w_pad = jnp.pad(router_w[0], ((0, 0), (0, LANES - n_experts)))
    router_b_pad = jnp.pad(router_b[0], (0, LANES - n_experts)).reshape(1, LANES)
    idx, gates = _router(h32, router_w_pad, router_b_pad, n_experts=n_experts)

    tm_g = MOE_TILE_ROWS if 2 * m // n_experts >= MOE_TILE_ROWS else 64
    n_tiles_max = (2 * m + n_experts * (tm_g - 1) + tm_g - 1) // tm_g
    dest, src_rows, tile_expert, n_used, tile_rows = _route_plan(
        idx, n_experts=n_experts, tm=tm_g, n_tiles_max=n_tiles_max)
    gather_rows = _div_tile(tm_g, 272, 16)
    xs = _gather_rows(h32, src_rows, n_used * (tm_g // gather_rows), rows=gather_rows)
    f_e = moe_w1.shape[-1]
    chunk = tm_g // 2
    act = _moe_gateup(xs, moe_w1, moe_w3, tile_expert, n_used, tile_rows, layer=0, tm=tm_g,
                      tn=_div_tile(f_e, 256, LANES), chunk=chunk)
    o_sorted = _moe_down(act, moe_w2, tile_expert, n_used, tile_rows, layer=0, tm=tm_g,
                         tn=_div_tile(d, 2048, LANES), tk=_div_tile(f_e, 1024, LANES), chunk=chunk)
    x = _moe_combine(o_sorted, dest, gates, x, norm_ffn_post[1])

    k_conf = conf_w.shape[-2]
    new_conf_p = glu_tail[:, CONF_HALO - (k_conf - 1):, :][None]
    new_conf_s = jnp.concatenate([state_conf[0][:, 1:, :], glu_new[:, None, :]], axis=1)[None]
    new_ckv_p = ckv[:m_p].reshape(1, n_seq, seq, kvl)
    new_ckv_s = ckv[m_p:].reshape(1, nb, 1, kvl)
    new_kpe_p = kpe[:m_p, :rope_dim].reshape(1, n_seq, seq, rope_dim)
    new_kpe_s = kpe[m_p:, :rope_dim].reshape(1, nb, 1, rope_dim)

    return (x[:m_p].reshape(n_seq, seq, d), x[m_p:].reshape(nb, 1, d),
            new_pool_p, new_pool_s, new_sconv_p, new_sconv_s, new_conf_p, new_conf_s,
            new_ckv_p, new_ckv_s, new_kpe_p, new_kpe_s)
```
